```python
import math
import jax, jax.numpy as jnp
from jax import lax
import numpy as np

D_MODEL = 2048
BATCH = 4
SEQ = 4096
DEPTH = 2

N_MEM = 256
BRANCH_WIDTH = 1024
N_BRANCH = 4
DA_HEADS = 8
DA_QK_DIM = 64
DA_V_DIM = 2 * DA_QK_DIM
SB_HEADS = 8
SB_HEAD_DIM = BRANCH_WIDTH // SB_HEADS
POOL_WINDOWS = (2, 4, 8, 16)
POOL_GROUPS = 4
POOL_GROUP = BRANCH_WIDTH // POOL_GROUPS
MEM_HEADS = 4
MEM_HEAD_DIM = BRANCH_WIDTH // MEM_HEADS
REL_BUCKETS = 32
REL_MAX_DIST = 128
Q_BLOCK = 128
EPS = 1e-6
N_SLICES = 12
IN_COLS = N_SLICES * BRANCH_WIDTH + N_BRANCH * D_MODEL

kernel_name = "hybrid_gated_diff_stickbreak_pool_mem"


def rmsnorm(x, g):
    xf = x.astype(jnp.float32)
    y = xf * lax.rsqrt(jnp.mean(xf * xf, axis=-1, keepdims=True) + EPS)
    return (y * g.astype(jnp.float32)).astype(x.dtype)


def t5_bucket(rel):
    n = jnp.maximum(rel, 0)
    max_exact = REL_BUCKETS // 2
    nf = jnp.maximum(n, 1).astype(jnp.float32)
    large = max_exact + (jnp.log(nf / max_exact) / math.log(REL_MAX_DIST / max_exact)
                         * (REL_BUCKETS - max_exact)).astype(jnp.int32)
    large = jnp.minimum(large, REL_BUCKETS - 1)
    return jnp.where(n < max_exact, n, large)


def to_blocks(t):
    b, s = t.shape[:2]
    t = t.reshape((b, s // Q_BLOCK, Q_BLOCK) + t.shape[2:])
    return jnp.moveaxis(t, 1, 0)


def from_blocks(t):
    t = jnp.moveaxis(t, 0, 1)
    return t.reshape((t.shape[0], -1) + t.shape[3:])


def diff_attention(q, k, v, rel_bias, lam):
    s_len = q.shape[1]
    k_pos = jnp.arange(s_len)
    scale = DA_QK_DIM ** -0.5

    def block(args):
        qb, start = args
        q_pos = start + jnp.arange(Q_BLOCK)
        rel = q_pos[:, None] - k_pos[None, :]
        bias = jnp.transpose(rel_bias[t5_bucket(rel)], (2, 0, 1)).astype(jnp.float32)
        logits = jnp.einsum('bqhcd,bkhcd->bhcqk', qb, k).astype(jnp.float32) * scale
        logits = logits + bias[None, :, None]
        logits = jnp.where(rel >= 0, logits, -jnp.inf)
        p = jax.nn.softmax(logits, axis=-1)
        a = p[:, :, 0] - lam * p[:, :, 1]
        return jnp.einsum('bhqk,bkhd->bqhd', a.astype(v.dtype), v)

    nb = s_len // Q_BLOCK
    out = lax.map(block, (to_blocks(q), jnp.arange(nb) * Q_BLOCK))
    return from_blocks(out)


def stick_breaking(q, k, v):
    s_len = q.shape[1]
    k_pos = jnp.arange(s_len)
    scale = SB_HEAD_DIM ** -0.5

    def block(args):
        qb, start = args
        q_pos = start + jnp.arange(Q_BLOCK)
        mask = k_pos[None, :] < q_pos[:, None]
        z = jnp.einsum('bqhd,bkhd->bhqk', qb, k).astype(jnp.float32) * scale
        log_beta = jax.nn.log_sigmoid(z)
        log_1mb = jnp.where(mask, jax.nn.log_sigmoid(-z), 0.0)
        between = lax.cumsum(log_1mb, axis=3, reverse=True) - log_1mb
        a = jnp.where(mask, jnp.exp(log_beta + between), 0.0)
        return jnp.einsum('bhqk,bkhd->bqhd', a.astype(v.dtype), v)

    nb = s_len // Q_BLOCK
    out = lax.map(block, (to_blocks(q), jnp.arange(nb) * Q_BLOCK))
    return from_blocks(out)


def multiscale_pool(u, w_pool, pool_scale):
    b, s, _ = u.shape
    ug = u.reshape(b, s, POOL_GROUPS, POOL_GROUP).astype(jnp.float32)
    c0 = jnp.concatenate([jnp.zeros((b, 1, POOL_GROUPS, POOL_GROUP), jnp.float32),
                          jnp.cumsum(ug, axis=1)], axis=1)
    t = jnp.arange(s)
    outs = []
    for g, w in enumerate(POOL_WINDOWS):
        cg = c0[:, :, g]
        lo = jnp.concatenate([jnp.zeros((b, w - 1, POOL_GROUP), jnp.float32),
                              cg[:, :s - w + 1]], axis=1)
        count = jnp.minimum(t + 1, w).astype(jnp.float32)[None, :, None]
        outs.append((cg[:, 1:] - lo) / count - ug[:, :, g])
    pooled = jnp.stack(outs, axis=2).astype(u.dtype)
    mixed = jnp.einsum('bsgc,gcd->bsgd', pooled, w_pool)
    return mixed.reshape(b, s, BRANCH_WIDTH) * pool_scale


def memory_attention(q, mk, mv):
    logits = jnp.einsum('bqhd,bmhd->bhqm', q, mk).astype(jnp.float32) * (MEM_HEAD_DIM ** -0.5)
    p = jax.nn.softmax(logits, axis=-1)
    return jnp.einsum('bhqm,bmhd->bqhd', p.astype(mv.dtype), mv)


def hybrid_layer(x, mem, layer_idx, rel_bias, norm_g, w_in, gate_b, lam_q1, lam_k1,
                 lam_q2, lam_k2, da_norm_g, w_pool, pool_scale, mem_norm_g, w_mem_kv,
                 w_branch, w_out):
    b, s, _ = x.shape
    h = rmsnorm(x, norm_g)
    proj = h @ w_in
    (da_q, da_k, da_v, da_z, sb_q, sb_k, sb_v, sb_z,
     pool_u, pool_z, mem_q, mem_z) = jnp.split(proj[..., :N_SLICES * BRANCH_WIDTH], N_SLICES, axis=-1)
    gate_logits = proj[..., N_SLICES * BRANCH_WIDTH:].reshape(b, s, N_BRANCH, D_MODEL) + gate_b

    lam_init = 0.8 - 0.6 * math.exp(-0.3 * layer_idx)
    lam = (jnp.exp(jnp.sum((lam_q1 * lam_k1).astype(jnp.float32)))
           - jnp.exp(jnp.sum((lam_q2 * lam_k2).astype(jnp.float32))) + lam_init)
    o_da = diff_attention(da_q.reshape(b, s, DA_HEADS, 2, DA_QK_DIM),
                          da_k.reshape(b, s, DA_HEADS, 2, DA_QK_DIM),
                          da_v.reshape(b, s, DA_HEADS, DA_V_DIM), rel_bias, lam)
    o_da = rmsnorm(o_da, da_norm_g.reshape(DA_HEADS, DA_V_DIM)) * (1.0 - lam_init)
    o_da = o_da.reshape(b, s, BRANCH_WIDTH)

    o_sb = stick_breaking(sb_q.reshape(b, s, SB_HEADS, SB_HEAD_DIM),
                          sb_k.reshape(b, s, SB_HEADS, SB_HEAD_DIM),
                          sb_v.reshape(b, s, SB_HEADS, SB_HEAD_DIM)).reshape(b, s, BRANCH_WIDTH)

    o_pool = multiscale_pool(pool_u, w_pool, pool_scale)

    mkv = rmsnorm(mem, mem_norm_g) @ w_mem_kv
    mk, mv = jnp.split(mkv, 2, axis=-1)
    m = mem.shape[1]
    o_mem = memory_attention(mem_q.reshape(b, s, MEM_HEADS, MEM_HEAD_DIM),
                             mk.reshape(b, m, MEM_HEADS, MEM_HEAD_DIM),
                             mv.reshape(b, m, MEM_HEADS, MEM_HEAD_DIM)).reshape(b, s, BRANCH_WIDTH)

    branches = (o_da * jax.nn.silu(da_z), o_sb * jax.nn.silu(sb_z),
                o_pool * jax.nn.silu(pool_z), o_mem * jax.nn.silu(mem_z))
    merged = jnp.zeros_like(x)
    for n in range(N_BRANCH):
        merged = merged + jax.nn.sigmoid(gate_logits[:, :, n]) * (branches[n] @ w_branch[n])
    return x + merged @ w_out


def setup_inputs(seed: int = 0) -> dict:
    key = jax.random.key(seed)
    ks = jax.random.split(key, 20)
    f32 = jnp.float32
    W = BRANCH_WIDTH
    nrm = lambda k, shape, s: jax.random.normal(k, shape, f32) * s
    return {
        "x": nrm(ks[0], (BATCH, SEQ, D_MODEL), 1.0),
        "mem": nrm(ks[1], (BATCH, N_MEM, D_MODEL), 1.0),
        "rel_bias": nrm(ks[2], (REL_BUCKETS, DA_HEADS), 0.5),
        "norm_g": 1.0 + nrm(ks[3], (DEPTH, D_MODEL), 0.02),
        "w_in": nrm(ks[4], (DEPTH, D_MODEL, IN_COLS), D_MODEL ** -0.5),
        "gate_b": nrm(ks[5], (DEPTH, N_BRANCH, D_MODEL), 0.02),
        "lam_q1": nrm(ks[6], (DEPTH, DA_QK_DIM), 0.1),
        "lam_k1": nrm(ks[7], (DEPTH, DA_QK_DIM), 0.1),
        "lam_q2": nrm(ks[8], (DEPTH, DA_QK_DIM), 0.1),
        "lam_k2": nrm(ks[9], (DEPTH, DA_QK_DIM), 0.1),
        "da_norm_g": 1.0 + nrm(ks[10], (DEPTH, W), 0.02),
        "w_pool": nrm(ks[11], (DEPTH, POOL_GROUPS, POOL_GROUP, POOL_GROUP), POOL_GROUP ** -0.5),
        "pool_scale": 1.0 + nrm(ks[12], (DEPTH, W), 0.1),
        "mem_norm_g": 1.0 + nrm(ks[13], (DEPTH, D_MODEL), 0.02),
        "w_mem_kv": nrm(ks[14], (DEPTH, D_MODEL, 2 * W), D_MODEL ** -0.5),
        "w_branch": nrm(ks[15], (DEPTH, N_BRANCH, W, D_MODEL), W ** -0.5),
        "w_out": nrm(ks[16], (DEPTH, D_MODEL, D_MODEL), D_MODEL ** -0.5),
        "final_g": 1.0 + nrm(ks[17], (D_MODEL,), 0.02),
    }


def reference(x, mem, rel_bias, norm_g, w_in, gate_b, lam_q1, lam_k1, lam_q2, lam_k2,
              da_norm_g, w_pool, pool_scale, mem_norm_g, w_mem_kv, w_branch, w_out, final_g):
    for l in range(DEPTH):
        x = hybrid_layer(x, mem, l, rel_bias, norm_g[l], w_in[l], gate_b[l], lam_q1[l], lam_k1[l],
                         lam_q2[l], lam_k2[l], da_norm_g[l], w_pool[l], pool_scale[l],
                         mem_norm_g[l], w_mem_kv[l], w_branch[l], w_out[l])
    return rmsnorm(x, final_g)
```

```python
import functools
import math

import jax
import jax.numpy as jnp
from jax import lax
from jax.experimental import pallas as pl
from jax.experimental.pallas import tpu as pltpu

F32 = jnp.float32
BF16 = jnp.bfloat16

BRANCH_WIDTH = 1024
N_BRANCH = 4
N_SLICES = 12
DA_HEADS = 8
DA_QK_DIM = 64
DA_V_DIM = 2 * DA_QK_DIM
SB_HEADS = 8
SB_HEAD_DIM = BRANCH_WIDTH // SB_HEADS
POOL_WINDOWS = (2, 4, 8, 16)
POOL_GROUP = BRANCH_WIDTH // len(POOL_WINDOWS)
MEM_HEADS = 4
MEM_HEAD_DIM = BRANCH_WIDTH // MEM_HEADS
REL_BUCKETS = 32
REL_MAX_DIST = 128
EPS = 1e-6

SL_DA_Q, SL_DA_K, SL_DA_V, SL_DA_Z = 0, 1, 2, 3
SL_SB_Q, SL_SB_K, SL_SB_V, SL_SB_Z = 4, 5, 6, 7
SL_POOL_U, SL_POOL_Z, SL_MEM_Q, SL_MEM_Z = 8, 9, 10, 11

V7X_VMEM_LIMIT_BYTES = 56 * 1024 * 1024
LANES = 128
POOL_HALO = 16
NEG_BIG = -1e30
ATTN_TILE = 256


def _params(*sem):
    return pltpu.CompilerParams(dimension_semantics=sem, vmem_limit_bytes=V7X_VMEM_LIMIT_BYTES)


def _tile(n, t):
    t = min(t, n)
    assert n % t == 0, (n, t)
    return t


def _silu(z):
    return z * (1.0 / (1.0 + jnp.exp(-z)))


def _dot_nt(a, b):
    return lax.dot_general(a, b, (((1,), (1,)), ((), ())), preferred_element_type=F32)


def _dot(a, b):
    return jnp.dot(a, b, preferred_element_type=F32)


def _rmsnorm_kernel(x_ref, g_ref, o_ref):
    x = x_ref[...]
    y = x * lax.rsqrt(jnp.mean(x * x, axis=-1, keepdims=True) + EPS)
    o_ref[...] = (y * g_ref[...]).astype(o_ref.dtype)


def rmsnorm(x2d, g, out_dtype, tr=512):
    n, d = x2d.shape
    tr = _tile(n, tr)
    return pl.pallas_call(
        _rmsnorm_kernel,
        grid=(n // tr,),
        in_specs=[pl.BlockSpec((tr, d), lambda i: (i, 0)), pl.BlockSpec((1, d), lambda i: (0, 0))],
        out_specs=pl.BlockSpec((tr, d), lambda i: (i, 0)),
        out_shape=jax.ShapeDtypeStruct((n, d), out_dtype),
        compiler_params=_params("parallel"),
        name="rmsnorm",
    )(x2d, g.reshape(1, d))


def _norm_matmul_kernel(x_ref, g_ref, w_ref, o_ref, h_scr):
    @pl.when(pl.program_id(1) == 0)
    def _():
        x = x_ref[...]
        y = x * lax.rsqrt(jnp.mean(x * x, axis=-1, keepdims=True) + EPS)
        h_scr[...] = (y * g_ref[...]).astype(BF16)

    o_ref[...] = _dot(h_scr[...], w_ref[...]).astype(o_ref.dtype)


def norm_matmul(x2d, g, w_bf16, out_dtype, tm=1024, tn=1024):
    m, d = x2d.shape
    n = w_bf16.shape[1]
    tm, tn = _tile(m, tm), _tile(n, tn)
    return pl.pallas_call(
        _norm_matmul_kernel,
        grid=(m // tm, n // tn),
        in_specs=[
            pl.BlockSpec((tm, d), lambda i, j: (i, 0)),
            pl.BlockSpec((1, d), lambda i, j: (0, 0)),
            pl.BlockSpec((d, tn), lambda i, j: (0, j)),
        ],
        out_specs=pl.BlockSpec((tm, tn), lambda i, j: (i, j)),
        out_shape=jax.ShapeDtypeStruct((m, n), out_dtype),
        scratch_shapes=[pltpu.VMEM((tm, d), BF16)],
        compiler_params=_params("parallel", "arbitrary"),
        name="norm_matmul",
    )(x2d, g.reshape(1, d), w_bf16)


def _matmul_residual_kernel(a_ref, w_ref, r_ref, o_ref):
    o_ref[...] = r_ref[...] + _dot(a_ref[...], w_ref[...])


def matmul_residual(a_bf16, w_bf16, res, tm=512):
    m, k = a_bf16.shape
    n = w_bf16.shape[1]
    tm = _tile(m, tm)
    return pl.pallas_call(
        _matmul_residual_kernel,
        grid=(m // tm,),
        in_specs=[
            pl.BlockSpec((tm, k), lambda i: (i, 0)),
            pl.BlockSpec((k, n), lambda i: (0, 0)),
            pl.BlockSpec((tm, n), lambda i: (i, 0)),
        ],
        out_specs=pl.BlockSpec((tm, n), lambda i: (i, 0)),
        out_shape=jax.ShapeDtypeStruct((m, n), F32),
        compiler_params=_params("parallel"),
        name="out_proj_residual",
    )(a_bf16, w_bf16, res)


def _bias_tiles_kernel(rb_ref, o_ref, *, t):
    h = pl.program_id(0)
    qi = lax.broadcasted_iota(jnp.int32, (t, t), 0)
    ki = lax.broadcasted_iota(jnp.int32, (t, t), 1)
    max_exact = REL_BUCKETS // 2
    far = rb_ref[REL_BUCKETS - 1, h]
    for sel in range(2):
        rel = qi - ki + sel * t
        n = jnp.maximum(rel, 0)
        nf = jnp.maximum(n, 1).astype(F32)
        large = max_exact + (jnp.log(nf / max_exact) / math.log(REL_MAX_DIST / max_exact)
                             * (REL_BUCKETS - max_exact)).astype(jnp.int32)
        large = jnp.minimum(large, REL_BUCKETS - 1)
        bucket = jnp.where(n < max_exact, n, large)
        val = jnp.zeros((t, t), F32)
        for b in range(REL_BUCKETS - 1):
            val = jnp.where(bucket == b, rb_ref[b, h] - far, val)
        if sel == 0:
            val = jnp.where(rel >= 0, val, NEG_BIG)
        o_ref[0, sel] = val


def bias_tiles(rel_bias, t):
    assert t >= REL_MAX_DIST
    heads = rel_bias.shape[1]
    return pl.pallas_call(
        functools.partial(_bias_tiles_kernel, t=t),
        grid=(heads,),
        in_specs=[pl.BlockSpec(memory_space=pltpu.SMEM)],
        out_specs=pl.BlockSpec((1, 2, t, t), lambda h: (h, 0, 0, 0)),
        out_shape=jax.ShapeDtypeStruct((heads, 2, t, t), F32),
        compiler_params=_params("parallel"),
        name="t5_bias_tiles",
    )(rel_bias)


def _da_kernel(lamv_ref, q_ref, k_ref, v_ref, z_ref, bias_ref, g_ref, o_ref,
               k_scr, v_scr, m_scr, l_scr, acc_scr, *, t, lam_init):
    qi = pl.program_id(2)

    @pl.when(qi == 0)
    def _():
        k_scr[...] = k_ref[0].astype(BF16)
        v_scr[...] = v_ref[0].astype(BF16)

    q = q_ref[0] * (DA_QK_DIM ** -0.5)
    lane = lax.broadcasted_iota(jnp.int32, q.shape, 1)
    qs = (jnp.where(lane < DA_QK_DIM, q, 0.0).astype(BF16), jnp.where(lane >= DA_QK_DIM, q, 0.0).astype(BF16))

    m_scr[...] = jnp.full(m_scr.shape, -jnp.inf, F32)
    l_scr[...] = jnp.zeros(l_scr.shape, F32)
    acc_scr[...] = jnp.zeros(acc_scr.shape, F32)

    def tile(j, bias):
        off = pl.multiple_of(j * t, t)
        k = k_scr[pl.ds(off, t), :]
        v = v_scr[pl.ds(off, t), :]
        for c in range(2):
            s = _dot_nt(qs[c], k)
            if bias is not None:
                s = s + bias
            m_prev = m_scr[c]
            m_new = jnp.maximum(m_prev, jnp.max(s, axis=-1, keepdims=True))
            alpha = jnp.exp(m_prev - m_new)
            p = jnp.exp(s - m_new)
            l_scr[c] = alpha * l_scr[c] + jnp.sum(p, axis=-1, keepdims=True)
            acc_scr[c] = alpha * acc_scr[c] + _dot(p.astype(BF16), v)
            m_scr[c] = m_new

    def far_body(j, carry):
        tile(j, None)
        return carry

    lax.fori_loop(0, jnp.maximum(qi - 1, 0), far_body, 0)

    @pl.when(qi >= 1)
    def _():
        tile(qi - 1, bias_ref[0, 1])

    tile(qi, bias_ref[0, 0])

    lv = lamv_ref[...]
    s1 = jnp.sum(lv[0:1] * lv[1:2], axis=-1, keepdims=True)
    s2 = jnp.sum(lv[2:3] * lv[3:4], axis=-1, keepdims=True)
    lam = jnp.exp(s1) - jnp.exp(s2) + lam_init
    o = acc_scr[0] / l_scr[0] - lam * (acc_scr[1] / l_scr[1])
    y = o * lax.rsqrt(jnp.mean(o * o, axis=-1, keepdims=True) + EPS)
    y = (y * g_ref[...]) * (1.0 - lam_init)
    o_ref[0] = (y * _silu(z_ref[0])).astype(o_ref.dtype)


def diff_attention_branch(proj, bias, lamv, da_norm_g, lam_init, t):
    b, s, _ = proj.shape
    t = _tile(s, t)
    hd = DA_V_DIM
    cpb = BRANCH_WIDTH // hd
    return pl.pallas_call(
        functools.partial(_da_kernel, t=t, lam_init=lam_init),
        grid=(b, DA_HEADS, s // t),
        in_specs=[
            pl.BlockSpec((4, DA_QK_DIM), lambda bi, h, qi: (0, 0)),
            pl.BlockSpec((1, t, hd), lambda bi, h, qi: (bi, qi, SL_DA_Q * cpb + h)),
            pl.BlockSpec((1, s, hd), lambda bi, h, qi: (bi, 0, SL_DA_K * cpb + h)),
            pl.BlockSpec((1, s, hd), lambda bi, h, qi: (bi, 0, SL_DA_V * cpb + h)),
            pl.BlockSpec((1, t, hd), lambda bi, h, qi: (bi, qi, SL_DA_Z * cpb + h)),
            pl.BlockSpec((1, 2, t, t), lambda bi, h, qi: (h, 0, 0, 0)),
            pl.BlockSpec((1, hd), lambda bi, h, qi: (0, h)),
        ],
        out_specs=pl.BlockSpec((1, t, hd), lambda bi, h, qi: (bi, qi, h)),
        out_shape=jax.ShapeDtypeStruct((b, s, BRANCH_WIDTH), BF16),
        scratch_shapes=[
            pltpu.VMEM((s, hd), BF16),
            pltpu.VMEM((s, hd), BF16),
            pltpu.VMEM((2, t, 1), F32),
            pltpu.VMEM((2, t, 1), F32),
            pltpu.VMEM((2, t, hd), F32),
        ],
        compiler_params=_params("parallel", "parallel", "arbitrary"),
        name="diff_attention",
    )(lamv, proj, proj, proj, proj, bias, da_norm_g.reshape(1, BRANCH_WIDTH))


def _sb_kernel(q_ref, k_ref, v_ref, z_ref, o_ref, k_scr, v_scr, c_scr, acc_scr, *, t):
    qi = pl.program_id(2)

    @pl.when(qi == 0)
    def _():
        k_scr[...] = k_ref[0].astype(BF16)
        v_scr[...] = v_ref[0].astype(BF16)

    q = q_ref[0].astype(BF16)
    row = lax.broadcasted_iota(jnp.int32, (t, t), 0)
    col = lax.broadcasted_iota(jnp.int32, (t, t), 1)
    upper = jnp.where(row > col, 1.0, 0.0).astype(BF16)
    causal = col < row
    scale = SB_HEAD_DIM ** -0.5

    c_scr[...] = jnp.zeros(c_scr.shape, F32)
    acc_scr[...] = jnp.zeros(acc_scr.shape, F32)

    def tile(j, diag):
        off = pl.multiple_of(j * t, t)
        k = k_scr[pl.ds(off, t), :]
        v = v_scr[pl.ds(off, t), :]
        z = _dot_nt(q, k) * scale
        log_beta = jnp.minimum(z, 0.0) - jnp.log1p(jnp.exp(-jnp.abs(z)))
        log_1mb = log_beta - z
        if diag:
            log_1mb = jnp.where(causal, log_1mb, 0.0)
        hi = log_1mb.astype(BF16)
        lo = (log_1mb - hi.astype(F32)).astype(BF16)
        between = c_scr[...] + (_dot(hi, upper) + _dot(lo, upper))
        a = jnp.exp(log_beta + between)
        if diag:
            a = jnp.where(causal, a, 0.0)
        acc_scr[...] += _dot(a.astype(BF16), v)
        c_scr[...] += jnp.sum(log_1mb, axis=-1, keepdims=True)

    tile(qi, True)

    def body(jj, carry):
        tile(qi - 1 - jj, False)
        return carry

    lax.fori_loop(0, qi, body, 0)
    o_ref[0] = (acc_scr[...] * _silu(z_ref[0])).astype(o_ref.dtype)


def stick_breaking_branch(proj, t):
    b, s, _ = proj.shape
    t = _tile(s, t)
    hd = SB_HEAD_DIM
    cpb = BRANCH_WIDTH // hd
    return pl.pallas_call(
        functools.partial(_sb_kernel, t=t),
        grid=(b, SB_HEADS, s // t),
        in_specs=[
            pl.BlockSpec((1, t, hd), lambda bi, h, qi: (bi, qi, SL_SB_Q * cpb + h)),
            pl.BlockSpec((1, s, hd), lambda bi, h, qi: (bi, 0, SL_SB_K * cpb + h)),
            pl.BlockSpec((1, s, hd), lambda bi, h, qi: (bi, 0, SL_SB_V * cpb + h)),
            pl.BlockSpec((1, t, hd), lambda bi, h, qi: (bi, qi, SL_SB_Z * cpb + h)),
        ],
        out_specs=pl.BlockSpec((1, t, hd), lambda bi, h, qi: (bi, qi, h)),
        out_shape=jax.ShapeDtypeStruct((b, s, BRANCH_WIDTH), BF16),
        scratch_shapes=[
            pltpu.VMEM((s, hd), BF16),
            pltpu.VMEM((s, hd), BF16),
            pltpu.VMEM((t, 1), F32),
            pltpu.VMEM((t, hd), F32),
        ],
        compiler_params=_params("parallel", "parallel", "arbitrary"),
        name="stick_breaking",
    )(proj, proj, proj, proj)


def _pool_kernel(u_ref, z_ref, w_ref, sc_ref, o_ref, ext_scr, *, tt):
    ti = pl.program_id(1)

    @pl.when(ti == 0)
    def _():
        ext_scr[0:POOL_HALO] = jnp.zeros((POOL_HALO, BRANCH_WIDTH), F32)

    ext_scr[POOL_HALO:POOL_HALO + tt] = u_ref[0]
    pos = ti * tt + lax.broadcasted_iota(jnp.int32, (tt, 1), 0)
    for g, w in enumerate(POOL_WINDOWS):
        cols = slice(g * POOL_GROUP, (g + 1) * POOL_GROUP)
        u = ext_scr[POOL_HALO:POOL_HALO + tt, cols]
        win = u
        for j in range(1, w):
            win = win + ext_scr[POOL_HALO - j:POOL_HALO - j + tt, cols]
        count = jnp.minimum(pos + 1, w).astype(F32)
        pooled = win / count - u
        mixed = _dot(pooled.astype(BF16), w_ref[g]) * sc_ref[:, cols]
        o_ref[0, :, cols] = (mixed * _silu(z_ref[0, :, cols])).astype(o_ref.dtype)
    ext_scr[0:POOL_HALO] = ext_scr[tt:tt + POOL_HALO]


def pool_branch(proj, w_pool_bf16, pool_scale, tt=512):
    b, s, _ = proj.shape
    tt = _tile(s, tt)
    assert tt >= POOL_HALO and max(POOL_WINDOWS) <= POOL_HALO
    return pl.pallas_call(
        functools.partial(_pool_kernel, tt=tt),
        grid=(b, s // tt),
        in_specs=[
            pl.BlockSpec((1, tt, BRANCH_WIDTH), lambda bi, ti: (bi, ti, SL_POOL_U)),
            pl.BlockSpec((1, tt, BRANCH_WIDTH), lambda bi, ti: (bi, ti, SL_POOL_Z)),
            pl.BlockSpec(w_pool_bf16.shape, lambda bi, ti: (0, 0, 0)),
            pl.BlockSpec((1, BRANCH_WIDTH), lambda bi, ti: (0, 0)),
        ],
        out_specs=pl.BlockSpec((1, tt, BRANCH_WIDTH), lambda bi, ti: (bi, ti, 0)),
        out_shape=jax.ShapeDtypeStruct((b, s, BRANCH_WIDTH), BF16),
        scratch_shapes=[pltpu.VMEM((tt + POOL_HALO, BRANCH_WIDTH), F32)],
        compiler_params=_params("parallel", "arbitrary"),
        name="multiscale_pool",
    )(proj, proj, w_pool_bf16, pool_scale.reshape(1, BRANCH_WIDTH))


def _mem_attn_kernel(q_ref, z_ref, mk_ref, mv_ref, o_ref):
    q = (q_ref[0] * (MEM_HEAD_DIM ** -0.5)).astype(BF16)
    s = _dot_nt(q, mk_ref[0].astype(BF16))
    p = jnp.exp(s - jnp.max(s, axis=-1, keepdims=True))
    l = jnp.sum(p, axis=-1, keepdims=True)
    o = _dot(p.astype(BF16), mv_ref[0].astype(BF16)) / l
    o_ref[0] = (o * _silu(z_ref[0])).astype(o_ref.dtype)


def memory_branch(proj, mkv, tq=512):
    b, s, _ = proj.shape
    m = mkv.shape[1]
    tq = _tile(s, tq)
    hd = MEM_HEAD_DIM
    cpb = BRANCH_WIDTH // hd
    return pl.pallas_call(
        _mem_attn_kernel,
        grid=(b, MEM_HEADS, s // tq),
        in_specs=[
            pl.BlockSpec((1, tq, hd), lambda bi, h, qi: (bi, qi, SL_MEM_Q * cpb + h)),
            pl.BlockSpec((1, tq, hd), lambda bi, h, qi: (bi, qi, SL_MEM_Z * cpb + h)),
            pl.BlockSpec((1, m, hd), lambda bi, h, qi: (bi, 0, h)),
            pl.BlockSpec((1, m, hd), lambda bi, h, qi: (bi, 0, cpb + h)),
        ],
        out_specs=pl.BlockSpec((1, tq, hd), lambda bi, h, qi: (bi, qi, h)),
        out_shape=jax.ShapeDtypeStruct((b, s, BRANCH_WIDTH), BF16),
        compiler_params=_params("parallel", "parallel", "parallel"),
        name="memory_attention",
    )(proj, proj, mkv, mkv)


def _merge_kernel(b0_ref, b1_ref, b2_ref, b3_ref, g0_ref, g1_ref, g2_ref, g3_ref, gb_ref, w_ref, o_ref):
    branches = (b0_ref, b1_ref, b2_ref, b3_ref)
    gates = (g0_ref, g1_ref, g2_ref, g3_ref)
    merged = None
    for n in range(N_BRANCH):
        gate = 1.0 / (1.0 + jnp.exp(-(gates[n][...] + gb_ref[n:n + 1, :])))
        term = gate * _dot(branches[n][...], w_ref[n])
        merged = term if merged is None else merged + term
    o_ref[...] = merged.astype(o_ref.dtype)


def gated_merge(branches, proj2d, gate_b, w_branch_bf16, tm=512, tn=512):
    m = proj2d.shape[0]
    d = w_branch_bf16.shape[2]
    tm, tn = _tile(m, tm), _tile(d, tn)
    gate_col0 = N_SLICES * BRANCH_WIDTH // tn
    per_gate = d // tn
    br_spec = pl.BlockSpec((tm, BRANCH_WIDTH), lambda i, j: (i, 0))

    def gate_spec(n):
        return pl.BlockSpec((tm, tn), lambda i, j: (i, gate_col0 + n * per_gate + j))

    return pl.pallas_call(
        _merge_kernel,
        grid=(m // tm, d // tn),
        in_specs=[br_spec] * N_BRANCH + [gate_spec(n) for n in range(N_BRANCH)] + [
            pl.BlockSpec((N_BRANCH, tn), lambda i, j: (0, j)),
            pl.BlockSpec((N_BRANCH, BRANCH_WIDTH, tn), lambda i, j: (0, 0, j)),
        ],
        out_specs=pl.BlockSpec((tm, tn), lambda i, j: (i, j)),
        out_shape=jax.ShapeDtypeStruct((m, d), BF16),
        compiler_params=_params("parallel", "arbitrary"),
        name="gated_merge",
    )(*branches, proj2d, proj2d, proj2d, proj2d, gate_b, w_branch_bf16)


def kernel(x, mem, rel_bias, norm_g, w_in, gate_b, lam_q1, lam_k1, lam_q2, lam_k2, da_norm_g, w_pool,
           pool_scale, mem_norm_g, w_mem_kv, w_branch, w_out, final_g):
    b, s, d = x.shape
    depth = norm_g.shape[0]
    n_mem = mem.shape[1]
    in_cols = w_in.shape[2]
    t = min(ATTN_TILE, s)

    bias = bias_tiles(rel_bias, t)
    x2d = x.reshape(b * s, d)
    mem2d = mem.reshape(b * n_mem, d)
    for l in range(depth):
        lam_init = 0.8 - 0.6 * math.exp(-0.3 * l)
        proj2d = norm_matmul(x2d, norm_g[l], w_in[l].astype(BF16), F32)
        proj = proj2d.reshape(b, s, in_cols)

        lamv = jnp.stack([lam_q1[l], lam_k1[l], lam_q2[l], lam_k2[l]])
        br_da = diff_attention_branch(proj, bias, lamv, da_norm_g[l], lam_init, t)
        br_sb = stick_breaking_branch(proj, t)
        br_pool = pool_branch(proj, w_pool[l].astype(BF16), pool_scale[l])
        mkv = norm_matmul(mem2d, mem_norm_g[l], w_mem_kv[l].astype(BF16), F32)
        br_mem = memory_branch(proj, mkv.reshape(b, n_mem, 2 * BRANCH_WIDTH))

        branches = [br.reshape(b * s, BRANCH_WIDTH) for br in (br_da, br_sb, br_pool, br_mem)]
        merged = gated_merge(branches, proj2d, gate_b[l], w_branch[l].astype(BF16))
        x2d = matmul_residual(merged, w_out[l].astype(BF16), x2d)
    return rmsnorm(x2d, final_g, F32).reshape(b, s, d)
```

```python
import functools
import math

import jax
import jax.numpy as jnp
from jax import lax
from jax.experimental import pallas as pl
from jax.experimental.pallas import tpu as pltpu

F32 = jnp.float32
BF16 = jnp.bfloat16

BRANCH_WIDTH = 1024
N_BRANCH = 4
N_SLICES = 12
DA_HEADS = 8
DA_QK_DIM = 64
DA_V_DIM = 2 * DA_QK_DIM
SB_HEADS = 8
SB_HEAD_DIM = BRANCH_WIDTH // SB_HEADS
POOL_WINDOWS = (2, 4, 8, 16)
POOL_GROUP = BRANCH_WIDTH // len(POOL_WINDOWS)
MEM_HEADS = 4
MEM_HEAD_DIM = BRANCH_WIDTH // MEM_HEADS
REL_BUCKETS = 32
REL_MAX_DIST = 128
EPS = 1e-6

SL_DA_Q, SL_DA_K, SL_DA_V, SL_DA_Z = 0, 1, 2, 3
SL_SB_Q, SL_SB_K, SL_SB_V, SL_SB_Z = 4, 5, 6, 7
SL_POOL_U, SL_POOL_Z, SL_MEM_Q, SL_MEM_Z = 8, 9, 10, 11

V7X_VMEM_LIMIT_BYTES = 56 * 1024 * 1024
LANES = 128
POOL_HALO = 16
NEG_BIG = -1e30
ATTN_TILE = 512


def _params(*sem):
    return pltpu.CompilerParams(dimension_semantics=sem, vmem_limit_bytes=V7X_VMEM_LIMIT_BYTES)


def _tile(n, t):
    t = min(t, n)
    assert n % t == 0, (n, t)
    return t


def _silu(z):
    return z * (1.0 / (1.0 + jnp.exp(-z)))


def _dot_nt(a, b):
    return lax.dot_general(a, b, (((1,), (1,)), ((), ())), preferred_element_type=F32)


def _dot(a, b):
    return jnp.dot(a, b, preferred_element_type=F32)


def _lane_tile(x, n):
    return x if n == 1 else jnp.concatenate([x] * n, axis=1)


def _rmsnorm_kernel(x_ref, g_ref, o_ref):
    x = x_ref[...]
    y = x * lax.rsqrt(jnp.mean(x * x, axis=-1, keepdims=True) + EPS)
    o_ref[...] = (y * g_ref[...]).astype(o_ref.dtype)


def rmsnorm(x2d, g, out_dtype, tr=512):
    n, d = x2d.shape
    tr = _tile(n, tr)
    return pl.pallas_call(
        _rmsnorm_kernel,
        grid=(n // tr,),
        in_specs=[pl.BlockSpec((tr, d), lambda i: (i, 0)), pl.BlockSpec((1, d), lambda i: (0, 0))],
        out_specs=pl.BlockSpec((tr, d), lambda i: (i, 0)),
        out_shape=jax.ShapeDtypeStruct((n, d), out_dtype),
        compiler_params=_params("parallel"),
        name="rmsnorm",
    )(x2d, g.reshape(1, d))


def _norm_matmul_kernel(x_ref, g_ref, w_ref, o_ref, h_scr):
    @pl.when(pl.program_id(1) == 0)
    def _():
        x = x_ref[...]
        y = x * lax.rsqrt(jnp.mean(x * x, axis=-1, keepdims=True) + EPS)
        h_scr[...] = (y * g_ref[...]).astype(BF16)

    o_ref[...] = _dot(h_scr[...], w_ref[...]).astype(o_ref.dtype)


def norm_matmul(x2d, g, w_bf16, out_dtype, tm=1024, tn=1024):
    m, d = x2d.shape
    n = w_bf16.shape[1]
    tm, tn = _tile(m, tm), _tile(n, tn)
    return pl.pallas_call(
        _norm_matmul_kernel,
        grid=(m // tm, n // tn),
        in_specs=[
            pl.BlockSpec((tm, d), lambda i, j: (i, 0)),
            pl.BlockSpec((1, d), lambda i, j: (0, 0)),
            pl.BlockSpec((d, tn), lambda i, j: (0, j)),
        ],
        out_specs=pl.BlockSpec((tm, tn), lambda i, j: (i, j)),
        out_shape=jax.ShapeDtypeStruct((m, n), out_dtype),
        scratch_shapes=[pltpu.VMEM((tm, d), BF16)],
        compiler_params=_params("parallel", "arbitrary"),
        name="norm_matmul",
    )(x2d, g.reshape(1, d), w_bf16)


def _matmul_residual_kernel(a_ref, w_ref, r_ref, o_ref):
    o_ref[...] = r_ref[...] + _dot(a_ref[...], w_ref[...])


def matmul_residual(a_bf16, w_bf16, res, tm=512):
    m, k = a_bf16.shape
    n = w_bf16.shape[1]
    tm = _tile(m, tm)
    return pl.pallas_call(
        _matmul_residual_kernel,
        grid=(m // tm,),
        in_specs=[
            pl.BlockSpec((tm, k), lambda i: (i, 0)),
            pl.BlockSpec((k, n), lambda i: (0, 0)),
            pl.BlockSpec((tm, n), lambda i: (i, 0)),
        ],
        out_specs=pl.BlockSpec((tm, n), lambda i: (i, 0)),
        out_shape=jax.ShapeDtypeStruct((m, n), F32),
        compiler_params=_params("parallel"),
        name="out_proj_residual",
    )(a_bf16, w_bf16, res)


def _bias_tiles_kernel(rb_ref, o_ref, *, t):
    h = pl.program_id(0)
    qi = lax.broadcasted_iota(jnp.int32, (t, t), 0)
    ki = lax.broadcasted_iota(jnp.int32, (t, t), 1)
    max_exact = REL_BUCKETS // 2
    far = rb_ref[REL_BUCKETS - 1, h]
    for sel in range(2):
        rel = qi - ki + sel * t
        n = jnp.maximum(rel, 0)
        nf = jnp.maximum(n, 1).astype(F32)
        large = max_exact + (jnp.log(nf / max_exact) / math.log(REL_MAX_DIST / max_exact)
                             * (REL_BUCKETS - max_exact)).astype(jnp.int32)
        large = jnp.minimum(large, REL_BUCKETS - 1)
        bucket = jnp.where(n < max_exact, n, large)
        val = jnp.zeros((t, t), F32)
        for b in range(REL_BUCKETS - 1):
            val = jnp.where(bucket == b, rb_ref[b, h] - far, val)
        if sel == 0:
            val = jnp.where(rel >= 0, val, NEG_BIG)
        o_ref[0, sel] = val


def bias_tiles(rel_bias, t):
    assert t >= REL_MAX_DIST
    heads = rel_bias.shape[1]
    return pl.pallas_call(
        functools.partial(_bias_tiles_kernel, t=t),
        grid=(heads,),
        in_specs=[pl.BlockSpec(memory_space=pltpu.SMEM)],
        out_specs=pl.BlockSpec((1, 2, t, t), lambda h: (h, 0, 0, 0)),
        out_shape=jax.ShapeDtypeStruct((heads, 2, t, t), F32),
        compiler_params=_params("parallel"),
        name="t5_bias_tiles",
    )(rel_bias)


def _da_kernel(lamv_ref, q_ref, k_ref, v_ref, z_ref, bias_ref, g_ref, o_ref,
               kt_scr, v_scr, q_scr, m_scr, acc_scr, *, t, lam_init):
    qi = pl.program_id(2)
    hd = DA_V_DIM

    @pl.when(qi == 0)
    def _():
        for jj in range(kt_scr.shape[0]):
            kt_scr[jj] = k_ref[0, jj * t:(jj + 1) * t, :].T.astype(BF16)
        v_scr[:, 0:hd] = v_ref[0].astype(BF16)
        v_scr[:, hd:2 * hd] = jnp.ones((v_scr.shape[0], hd), BF16)

    q = q_ref[0] * (DA_QK_DIM ** -0.5)
    lane = lax.broadcasted_iota(jnp.int32, q.shape, 1)
    q_scr[0] = jnp.where(lane < DA_QK_DIM, q, 0.0).astype(BF16)
    q_scr[1] = jnp.where(lane >= DA_QK_DIM, q, 0.0).astype(BF16)
    m_scr[...] = jnp.full(m_scr.shape, -jnp.inf, F32)
    acc_scr[...] = jnp.zeros(acc_scr.shape, F32)

    def tile(j, bias):
        kt = kt_scr[j]
        off = pl.multiple_of(j * t, t)
        v = v_scr[pl.ds(off, t), :]
        for c in range(2):
            s = _dot(q_scr[c], kt)
            if bias is not None:
                s = s + bias
            m_prev = m_scr[c]
            m_new = jnp.maximum(m_prev, jnp.max(s, axis=-1, keepdims=True))
            alpha = jnp.exp(m_prev - m_new)
            p = jnp.exp(s - _lane_tile(m_new, t // LANES))
            acc_scr[c] = _lane_tile(alpha, 2) * acc_scr[c] + _dot(p.astype(BF16), v)
            m_scr[c] = m_new

    def far_body(j, carry):
        tile(j, None)
        return carry

    lax.fori_loop(0, jnp.maximum(qi - 1, 0), far_body, 0)

    @pl.when(qi >= 1)
    def _():
        tile(qi - 1, bias_ref[0, 1])

    tile(qi, bias_ref[0, 0])

    lv = lamv_ref[...]
    s1 = jnp.sum(lv[0:1] * lv[1:2], axis=-1, keepdims=True)
    s2 = jnp.sum(lv[2:3] * lv[3:4], axis=-1, keepdims=True)
    lam = jnp.exp(s1) - jnp.exp(s2) + lam_init
    a0 = acc_scr[0]
    a1 = acc_scr[1]
    o = a0[:, 0:hd] / a0[:, hd:2 * hd] - lam * (a1[:, 0:hd] / a1[:, hd:2 * hd])
    y = o * lax.rsqrt(jnp.mean(o * o, axis=-1, keepdims=True) + EPS)
    y = (y * g_ref[...]) * (1.0 - lam_init)
    o_ref[0] = (y * _silu(z_ref[0])).astype(o_ref.dtype)


def diff_attention_branch(proj, bias, lamv, da_norm_g, lam_init, t):
    b, s, _ = proj.shape
    t = _tile(s, t)
    hd = DA_V_DIM
    cpb = BRANCH_WIDTH // hd
    return pl.pallas_call(
        functools.partial(_da_kernel, t=t, lam_init=lam_init),
        grid=(b, DA_HEADS, s // t),
        in_specs=[
            pl.BlockSpec((4, DA_QK_DIM), lambda bi, h, qi: (0, 0)),
            pl.BlockSpec((1, t, hd), lambda bi, h, qi: (bi, qi, SL_DA_Q * cpb + h)),
            pl.BlockSpec((1, s, hd), lambda bi, h, qi: (bi, 0, SL_DA_K * cpb + h)),
            pl.BlockSpec((1, s, hd), lambda bi, h, qi: (bi, 0, SL_DA_V * cpb + h)),
            pl.BlockSpec((1, t, hd), lambda bi, h, qi: (bi, qi, SL_DA_Z * cpb + h)),
            pl.BlockSpec((1, 2, t, t), lambda bi, h, qi: (h, 0, 0, 0)),
            pl.BlockSpec((1, hd), lambda bi, h, qi: (0, h)),
        ],
        out_specs=pl.BlockSpec((1, t, hd), lambda bi, h, qi: (bi, qi, h)),
        out_shape=jax.ShapeDtypeStruct((b, s, BRANCH_WIDTH), BF16),
        scratch_shapes=[
            pltpu.VMEM((s // t, hd, t), BF16),
            pltpu.VMEM((s, 2 * hd), BF16),
            pltpu.VMEM((2, t, hd), BF16),
            pltpu.VMEM((2, t, LANES), F32),
            pltpu.VMEM((2, t, 2 * hd), F32),
        ],
        compiler_params=_params("parallel", "parallel", "arbitrary"),
        name="diff_attention",
    )(lamv, proj, proj, proj, proj, bias, da_norm_g.reshape(1, BRANCH_WIDTH))


def _sb_kernel(q_ref, k_ref, v_ref, z_ref, o_ref, kt_scr, v_scr, q_scr, c_scr, acc_scr, *, t):
    qi = pl.program_id(2)

    @pl.when(qi == 0)
    def _():
        for jj in range(kt_scr.shape[0]):
            kt_scr[jj] = k_ref[0, jj * t:(jj + 1) * t, :].T.astype(BF16)
        v_scr[...] = v_ref[0].astype(BF16)

    q_scr[...] = q_ref[0].astype(BF16)
    row = lax.broadcasted_iota(jnp.int32, (t, t), 0)
    col = lax.broadcasted_iota(jnp.int32, (t, t), 1)
    scale = SB_HEAD_DIM ** -0.5

    c_scr[...] = jnp.zeros(c_scr.shape, F32)
    acc_scr[...] = jnp.zeros(acc_scr.shape, F32)

    def tile(j, diag):
        off = pl.multiple_of(j * t, t)
        v = v_scr[pl.ds(off, t), :]
        z = _dot(q_scr[...], kt_scr[j]) * scale
        log_beta = jnp.minimum(z, 0.0) - jnp.log1p(jnp.exp(-jnp.abs(z)))
        log_1mb = log_beta - z
        if diag:
            causal = col < row
            log_1mb = jnp.where(causal, log_1mb, 0.0)
        upper = jnp.where(row > col, 1.0, 0.0).astype(BF16)
        hi = log_1mb.astype(BF16)
        lo = (log_1mb - hi.astype(F32)).astype(BF16)
        between = _lane_tile(c_scr[...], t // LANES) + (_dot(hi, upper) + _dot(lo, upper))
        a = jnp.exp(log_beta + between)
        if diag:
            a = jnp.where(causal, a, 0.0)
        acc_scr[...] += _dot(a.astype(BF16), v)
        c_scr[...] += jnp.sum(log_1mb, axis=-1, keepdims=True)

    tile(qi, True)

    def body(jj, carry):
        tile(qi - 1 - jj, False)
        return carry

    lax.fori_loop(0, qi, body, 0)
    o_ref[0] = (acc_scr[...] * _silu(z_ref[0])).astype(o_ref.dtype)


def stick_breaking_branch(proj, t):
    b, s, _ = proj.shape
    t = _tile(s, t)
    hd = SB_HEAD_DIM
    cpb = BRANCH_WIDTH // hd
    return pl.pallas_call(
        functools.partial(_sb_kernel, t=t),
        grid=(b, SB_HEADS, s // t),
        in_specs=[
            pl.BlockSpec((1, t, hd), lambda bi, h, qi: (bi, qi, SL_SB_Q * cpb + h)),
            pl.BlockSpec((1, s, hd), lambda bi, h, qi: (bi, 0, SL_SB_K * cpb + h)),
            pl.BlockSpec((1, s, hd), lambda bi, h, qi: (bi, 0, SL_SB_V * cpb + h)),
            pl.BlockSpec((1, t, hd), lambda bi, h, qi: (bi, qi, SL_SB_Z * cpb + h)),
        ],
        out_specs=pl.BlockSpec((1, t, hd), lambda bi, h, qi: (bi, qi, h)),
        out_shape=jax.ShapeDtypeStruct((b, s, BRANCH_WIDTH), BF16),
        scratch_shapes=[
            pltpu.VMEM((s // t, hd, t), BF16),
            pltpu.VMEM((s, hd), BF16),
            pltpu.VMEM((t, hd), BF16),
            pltpu.VMEM((t, LANES), F32),
            pltpu.VMEM((t, hd), F32),
        ],
        compiler_params=_params("parallel", "parallel", "arbitrary"),
        name="stick_breaking",
    )(proj, proj, proj, proj)


def _pool_kernel(u_ref, z_ref, w_ref, sc_ref, o_ref, ext_scr, *, tt):
    ti = pl.program_id(1)

    @pl.when(ti == 0)
    def _():
        ext_scr[0:POOL_HALO] = jnp.zeros((POOL_HALO, BRANCH_WIDTH), F32)

    ext_scr[POOL_HALO:POOL_HALO + tt] = u_ref[0]
    pos = ti * tt + lax.broadcasted_iota(jnp.int32, (tt, 1), 0)
    for g, w in enumerate(POOL_WINDOWS):
        cols = slice(g * POOL_GROUP, (g + 1) * POOL_GROUP)
        u = ext_scr[POOL_HALO:POOL_HALO + tt, cols]
        win = u
        for j in range(1, w):
            win = win + ext_scr[POOL_HALO - j:POOL_HALO - j + tt, cols]
        count = jnp.minimum(pos + 1, w).astype(F32)
        pooled = win / count - u
        mixed = _dot(pooled.astype(BF16), w_ref[g]) * sc_ref[:, cols]
        o_ref[0, :, cols] = (mixed * _silu(z_ref[0, :, cols])).astype(o_ref.dtype)
    ext_scr[0:POOL_HALO] = ext_scr[tt:tt + POOL_HALO]


def pool_branch(proj, w_pool_bf16, pool_scale, tt=512):
    b, s, _ = proj.shape
    tt = _tile(s, tt)
    assert tt >= POOL_HALO and max(POOL_WINDOWS) <= POOL_HALO
    return pl.pallas_call(
        functools.partial(_pool_kernel, tt=tt),
        grid=(b, s // tt),
        in_specs=[
            pl.BlockSpec((1, tt, BRANCH_WIDTH), lambda bi, ti: (bi, ti, SL_POOL_U)),
            pl.BlockSpec((1, tt, BRANCH_WIDTH), lambda bi, ti: (bi, ti, SL_POOL_Z)),
            pl.BlockSpec(w_pool_bf16.shape, lambda bi, ti: (0, 0, 0)),
            pl.BlockSpec((1, BRANCH_WIDTH), lambda bi, ti: (0, 0)),
        ],
        out_specs=pl.BlockSpec((1, tt, BRANCH_WIDTH), lambda bi, ti: (bi, ti, 0)),
        out_shape=jax.ShapeDtypeStruct((b, s, BRANCH_WIDTH), BF16),
        scratch_shapes=[pltpu.VMEM((tt + POOL_HALO, BRANCH_WIDTH), F32)],
        compiler_params=_params("parallel", "arbitrary"),
        name="multiscale_pool",
    )(proj, proj, w_pool_bf16, pool_scale.reshape(1, BRANCH_WIDTH))


def _mem_attn_kernel(q_ref, z_ref, mk_ref, mv_ref, o_ref):
    q = (q_ref[0] * (MEM_HEAD_DIM ** -0.5)).astype(BF16)
    s = _dot_nt(q, mk_ref[0].astype(BF16))
    p = jnp.exp(s - jnp.max(s, axis=-1, keepdims=True))
    l = jnp.sum(p, axis=-1, keepdims=True)
    o = _dot(p.astype(BF16), mv_ref[0].astype(BF16)) / l
    o_ref[0] = (o * _silu(z_ref[0])).astype(o_ref.dtype)


def memory_branch(proj, mkv, tq=512):
    b, s, _ = proj.shape
    m = mkv.shape[1]
    tq = _tile(s, tq)
    hd = MEM_HEAD_DIM
    cpb = BRANCH_WIDTH // hd
    return pl.pallas_call(
        _mem_attn_kernel,
        grid=(b, MEM_HEADS, s // tq),
        in_specs=[
            pl.BlockSpec((1, tq, hd), lambda bi, h, qi: (bi, qi, SL_MEM_Q * cpb + h)),
            pl.BlockSpec((1, tq, hd), lambda bi, h, qi: (bi, qi, SL_MEM_Z * cpb + h)),
            pl.BlockSpec((1, m, hd), lambda bi, h, qi: (bi, 0, h)),
            pl.BlockSpec((1, m, hd), lambda bi, h, qi: (bi, 0, cpb + h)),
        ],
        out_specs=pl.BlockSpec((1, tq, hd), lambda bi, h, qi: (bi, qi, h)),
        out_shape=jax.ShapeDtypeStruct((b, s, BRANCH_WIDTH), BF16),
        compiler_params=_params("parallel", "parallel", "parallel"),
        name="memory_attention",
    )(proj, proj, mkv, mkv)


def _merge_kernel(b0_ref, b1_ref, b2_ref, b3_ref, g0_ref, g1_ref, g2_ref, g3_ref, gb_ref, w_ref, o_ref):
    branches = (b0_ref, b1_ref, b2_ref, b3_ref)
    gates = (g0_ref, g1_ref, g2_ref, g3_ref)
    merged = None
    for n in range(N_BRANCH):
        gate = 1.0 / (1.0 + jnp.exp(-(gates[n][...] + gb_ref[n:n + 1, :])))
        term = gate * _dot(branches[n][...], w_ref[n])
        merged = term if merged is None else merged + term
    o_ref[...] = merged.astype(o_ref.dtype)


def gated_merge(branches, proj2d, gate_b, w_branch_bf16, tm=512, tn=512):
    m = proj2d.shape[0]
    d = w_branch_bf16.shape[2]
    tm, tn = _tile(m, tm), _tile(d, tn)
    gate_col0 = N_SLICES * BRANCH_WIDTH // tn
    per_gate = d // tn
    br_spec = pl.BlockSpec((tm, BRANCH_WIDTH), lambda i, j: (i, 0))

    def gate_spec(n):
        return pl.BlockSpec((tm, tn), lambda i, j: (i, gate_col0 + n * per_gate + j))

    return pl.pallas_call(
        _merge_kernel,
        grid=(m // tm, d // tn),
        in_specs=[br_spec] * N_BRANCH + [gate_spec(n) for n in range(N_BRANCH)] + [
            pl.BlockSpec((N_BRANCH, tn), lambda i, j: (0, j)),
            pl.BlockSpec((N_BRANCH, BRANCH_WIDTH, tn), lambda i, j: (0, 0, j)),
        ],
        out_specs=pl.BlockSpec((tm, tn), lambda i, j: (i, j)),
        out_shape=jax.ShapeDtypeStruct((m, d), BF16),
        compiler_params=_params("parallel", "arbitrary"),
        name="gated_merge",
    )(*branches, proj2d, proj2d, proj2d, proj2d, gate_b, w_branch_bf16)


def kernel(x, mem, rel_bias, norm_g, w_in, gate_b, lam_q1, lam_k1, lam_q2, lam_k2, da_norm_g, w_pool,
           pool_scale, mem_norm_g, w_mem_kv, w_branch, w_out, final_g):
    b, s, d = x.shape
    depth = norm_g.shape[0]
    n_mem = mem.shape[1]
    in_cols = w_in.shape[2]
    t = min(ATTN_TILE, s)

    bias = bias_tiles(rel_bias, t)
    x2d = x.reshape(b * s, d)
    mem2d = mem.reshape(b * n_mem, d)
    for l in range(depth):
        lam_init = 0.8 - 0.6 * math.exp(-0.3 * l)
        proj2d = norm_matmul(x2d, norm_g[l], w_in[l].astype(BF16), F32)
        proj = proj2d.reshape(b, s, in_cols)

        lamv = jnp.stack([lam_q1[l], lam_k1[l], lam_q2[l], lam_k2[l]])
        br_da = diff_attention_branch(proj, bias, lamv, da_norm_g[l], lam_init, t)
        br_sb = stick_breaking_branch(proj, t)
        br_pool = pool_branch(proj, w_pool[l].astype(BF16), pool_scale[l])
        mkv = norm_matmul(mem2d, mem_norm_g[l], w_mem_kv[l].astype(BF16), F32)
        br_mem = memory_branch(proj, mkv.reshape(b, n_mem, 2 * BRANCH_WIDTH))

        branches = [br.reshape(b * s, BRANCH_WIDTH) for br in (br_da, br_sb, br_pool, br_mem)]
        merged = gated_merge(branches, proj2d, gate_b[l], w_branch[l].astype(BF16))
        x2d = matmul_residual(merged, w_out[l].astype(BF16), x2d)
    return rmsnorm(x2d, final_g, F32).reshape(b, s, d)
```

```python
import functools
import math

import jax
import jax.numpy as jnp
from jax import lax
from jax.experimental import pallas as pl
from jax.experimental.pallas import tpu as pltpu

F32 = jnp.float32
BF16 = jnp.bfloat16

BRANCH_WIDTH = 1024
N_BRANCH = 4
N_SLICES = 12
DA_HEADS = 8
DA_QK_DIM = 64
DA_V_DIM = 2 * DA_QK_DIM
SB_HEADS = 8
SB_HEAD_DIM = BRANCH_WIDTH // SB_HEADS
POOL_WINDOWS = (2, 4, 8, 16)
POOL_GROUP = BRANCH_WIDTH // len(POOL_WINDOWS)
MEM_HEADS = 4
MEM_HEAD_DIM = BRANCH_WIDTH // MEM_HEADS
REL_BUCKETS = 32
REL_MAX_DIST = 128
EPS = 1e-6

SL_DA_Q, SL_DA_K, SL_DA_V, SL_DA_Z = 0, 1, 2, 3
SL_SB_Q, SL_SB_K, SL_SB_V, SL_SB_Z = 4, 5, 6, 7
SL_POOL_U, SL_POOL_Z, SL_MEM_Q, SL_MEM_Z = 8, 9, 10, 11
QKV_SLICES = (SL_DA_Q, SL_DA_K, SL_DA_V, SL_SB_Q, SL_SB_K, SL_SB_V, SL_MEM_Q)
REST_SLICES = (SL_DA_Z, SL_SB_Z, SL_POOL_U, SL_POOL_Z, SL_MEM_Z)
QKV_DA_Q, QKV_DA_K, QKV_DA_V, QKV_SB_Q, QKV_SB_K, QKV_SB_V, QKV_MEM_Q = range(len(QKV_SLICES))
REST_DA_Z, REST_SB_Z, REST_POOL_U, REST_POOL_Z, REST_MEM_Z, REST_GATES = range(len(REST_SLICES) + 1)

V7X_VMEM_LIMIT_BYTES = 56 * 1024 * 1024
LANES = 128
POOL_HALO = 16
NEG_BIG = -1e30
ATTN_TILE = 512
ATTN_HEADS_PER_STEP = 2
SB_CUMSUM_BLOCK = 256
LOG2E = 1.4426950408889634
LN2 = 0.6931471805599453


def _params(*sem):
    return pltpu.CompilerParams(dimension_semantics=sem, vmem_limit_bytes=V7X_VMEM_LIMIT_BYTES)


def _tile(n, t):
    t = min(t, n)
    assert n % t == 0, (n, t)
    return t


def _silu(z):
    return z * (1.0 / (1.0 + jnp.exp(-z)))


def _dot_nt(a, b):
    return lax.dot_general(a, b, (((1,), (1,)), ((), ())), preferred_element_type=F32)


def _dot(a, b):
    return jnp.dot(a, b, preferred_element_type=F32)


def _lane_tile(x, n):
    return x if n == 1 else jnp.concatenate([x] * n, axis=1)


def _neg_abs(x):
    u = lax.bitcast_convert_type(x, jnp.uint32) | jnp.uint32(0x80000000)
    return lax.bitcast_convert_type(u, F32)


def _rmsnorm_kernel(x_ref, g_ref, o_ref):
    x = x_ref[...]
    y = x * lax.rsqrt(jnp.mean(x * x, axis=-1, keepdims=True) + EPS)
    o_ref[...] = (y * g_ref[...]).astype(o_ref.dtype)


def rmsnorm(x2d, g, out_dtype, tr=512):
    n, d = x2d.shape
    tr = _tile(n, tr)
    return pl.pallas_call(
        _rmsnorm_kernel,
        grid=(n // tr,),
        in_specs=[pl.BlockSpec((tr, d), lambda i: (i, 0)), pl.BlockSpec((1, d), lambda i: (0, 0))],
        out_specs=pl.BlockSpec((tr, d), lambda i: (i, 0)),
        out_shape=jax.ShapeDtypeStruct((n, d), out_dtype),
        compiler_params=_params("parallel"),
        name="rmsnorm",
    )(x2d, g.reshape(1, d))


def _norm_matmul_kernel(x_ref, g_ref, w_ref, o_ref, h_scr):
    @pl.when(pl.program_id(1) == 0)
    def _():
        x = x_ref[...]
        y = x * lax.rsqrt(jnp.mean(x * x, axis=-1, keepdims=True) + EPS)
        h_scr[...] = (y * g_ref[...]).astype(BF16)

    o_ref[...] = _dot(h_scr[...], w_ref[...]).astype(o_ref.dtype)


def norm_matmul(x2d, g, w_bf16, out_dtype, tm=1024, tn=1024):
    m, d = x2d.shape
    n = w_bf16.shape[1]
    tm, tn = _tile(m, tm), _tile(n, tn)
    return pl.pallas_call(
        _norm_matmul_kernel,
        grid=(m // tm, n // tn),
        in_specs=[
            pl.BlockSpec((tm, d), lambda i, j: (i, 0)),
            pl.BlockSpec((1, d), lambda i, j: (0, 0)),
            pl.BlockSpec((d, tn), lambda i, j: (0, j)),
        ],
        out_specs=pl.BlockSpec((tm, tn), lambda i, j: (i, j)),
        out_shape=jax.ShapeDtypeStruct((m, n), out_dtype),
        scratch_shapes=[pltpu.VMEM((tm, d), BF16)],
        compiler_params=_params("parallel", "arbitrary"),
        name="norm_matmul",
    )(x2d, g.reshape(1, d), w_bf16)


def _matmul_residual_kernel(a_ref, w_ref, r_ref, o_ref):
    o_ref[...] = r_ref[...] + _dot(a_ref[...], w_ref[...])


def matmul_residual(a_bf16, w_bf16, res, tm=512):
    m, k = a_bf16.shape
    n = w_bf16.shape[1]
    tm = _tile(m, tm)
    return pl.pallas_call(
        _matmul_residual_kernel,
        grid=(m // tm,),
        in_specs=[
            pl.BlockSpec((tm, k), lambda i: (i, 0)),
            pl.BlockSpec((k, n), lambda i: (0, 0)),
            pl.BlockSpec((tm, n), lambda i: (i, 0)),
        ],
        out_specs=pl.BlockSpec((tm, n), lambda i: (i, 0)),
        out_shape=jax.ShapeDtypeStruct((m, n), F32),
        compiler_params=_params("parallel"),
        name="out_proj_residual",
    )(a_bf16, w_bf16, res)


def _bias_tiles_kernel(rb_ref, o_ref, *, t):
    h = pl.program_id(0)
    qi = lax.broadcasted_iota(jnp.int32, (t, t), 0)
    ki = lax.broadcasted_iota(jnp.int32, (t, t), 1)
    max_exact = REL_BUCKETS // 2
    far = rb_ref[REL_BUCKETS - 1, h]
    for sel in range(2):
        rel = qi - ki + sel * t
        n = jnp.maximum(rel, 0)
        nf = jnp.maximum(n, 1).astype(F32)
        large = max_exact + (jnp.log(nf / max_exact) / math.log(REL_MAX_DIST / max_exact)
                             * (REL_BUCKETS - max_exact)).astype(jnp.int32)
        large = jnp.minimum(large, REL_BUCKETS - 1)
        bucket = jnp.where(n < max_exact, n, large)
        val = jnp.zeros((t, t), F32)
        for b in range(REL_BUCKETS - 1):
            val = jnp.where(bucket == b, rb_ref[b, h] - far, val)
        if sel == 0:
            val = jnp.where(rel >= 0, val, NEG_BIG)
        o_ref[0, sel] = val


def bias_tiles(rel_bias, t):
    assert t >= REL_MAX_DIST
    heads = rel_bias.shape[1]
    return pl.pallas_call(
        functools.partial(_bias_tiles_kernel, t=t),
        grid=(heads,),
        in_specs=[pl.BlockSpec(memory_space=pltpu.SMEM)],
        out_specs=pl.BlockSpec((1, 2, t, t), lambda h: (h, 0, 0, 0)),
        out_shape=jax.ShapeDtypeStruct((heads, 2, t, t), F32),
        compiler_params=_params("parallel"),
        name="t5_bias_tiles",
    )(rel_bias)


def _da_kernel(lamv_ref, q_ref, k_ref, v_ref, z_ref, bias_ref, g_ref, o_ref,
               kt_scr, v_scr, q_scr, m_scr, acc_scr, *, t, hp, lam_init):
    qi = pl.program_id(2)
    nt = kt_scr.shape[1]
    hd = DA_V_DIM

    @pl.when(qi == 0)
    def _():
        for h in range(hp):
            for jj in range(nt):
                kt_scr[h, jj] = k_ref[0, jj * t:(jj + 1) * t, h * hd:(h + 1) * hd].astype(F32).T.astype(BF16)
            v_scr[h, :, 0:hd] = v_ref[0, :, h * hd:(h + 1) * hd]
            v_scr[h, :, hd:2 * hd] = jnp.ones((v_scr.shape[1], hd), BF16)

    lane = lax.broadcasted_iota(jnp.int32, (t, hd), 1)
    for h in range(hp):
        q = q_ref[0, :, h * hd:(h + 1) * hd] * (DA_QK_DIM ** -0.5)
        q_scr[h, 0] = jnp.where(lane < DA_QK_DIM, q, 0.0).astype(BF16)
        q_scr[h, 1] = jnp.where(lane >= DA_QK_DIM, q, 0.0).astype(BF16)
    m_scr[...] = jnp.full(m_scr.shape, -jnp.inf, F32)
    acc_scr[...] = jnp.zeros(acc_scr.shape, F32)

    def tile(j, bias_sel):
        off = pl.multiple_of(j * t, t)
        for h in range(hp):
            kt = kt_scr[h, j]
            v = v_scr[h, pl.ds(off, t), :]
            for c in range(2):
                s = _dot(q_scr[h, c], kt)
                if bias_sel is not None:
                    s = s + bias_ref[h, bias_sel]
                m_prev = m_scr[h, c]
                m_new = jnp.maximum(m_prev, jnp.max(s, axis=-1, keepdims=True))
                alpha = jnp.exp(m_prev - m_new)
                p = jnp.exp(s - _lane_tile(m_new, t // LANES))
                acc_scr[h, c] = _lane_tile(alpha, 2) * acc_scr[h, c] + _dot(p.astype(BF16), v)
                m_scr[h, c] = m_new

    def far_body(j, carry):
        tile(j, None)
        return carry

    lax.fori_loop(0, jnp.maximum(qi - 1, 0), far_body, 0)

    @pl.when(qi >= 1)
    def _():
        tile(qi - 1, 1)

    tile(qi, 0)

    lv = lamv_ref[...]
    s1 = jnp.sum(lv[0:1] * lv[1:2], axis=-1, keepdims=True)
    s2 = jnp.sum(lv[2:3] * lv[3:4], axis=-1, keepdims=True)
    lam = jnp.exp(s1) - jnp.exp(s2) + lam_init
    for h in range(hp):
        a0 = acc_scr[h, 0]
        a1 = acc_scr[h, 1]
        o = a0[:, 0:hd] / a0[:, hd:2 * hd] - lam * (a1[:, 0:hd] / a1[:, hd:2 * hd])
        y = o * lax.rsqrt(jnp.mean(o * o, axis=-1, keepdims=True) + EPS)
        y = (y * g_ref[:, h * hd:(h + 1) * hd]) * (1.0 - lam_init)
        o_ref[0, :, h * hd:(h + 1) * hd] = (y * _silu(z_ref[0, :, h * hd:(h + 1) * hd])).astype(o_ref.dtype)


def diff_attention_branch(qkv, rest, bias, lamv, da_norm_g, lam_init, t, hp):
    b, s, _ = qkv.shape
    t = _tile(s, t)
    hd = DA_V_DIM
    w = hp * hd
    cpb = BRANCH_WIDTH // w
    return pl.pallas_call(
        functools.partial(_da_kernel, t=t, hp=hp, lam_init=lam_init),
        grid=(b, DA_HEADS // hp, s // t),
        in_specs=[
            pl.BlockSpec((4, DA_QK_DIM), lambda bi, h, qi: (0, 0)),
            pl.BlockSpec((1, t, w), lambda bi, h, qi: (bi, qi, QKV_DA_Q * cpb + h)),
            pl.BlockSpec((1, s, w), lambda bi, h, qi: (bi, 0, QKV_DA_K * cpb + h)),
            pl.BlockSpec((1, s, w), lambda bi, h, qi: (bi, 0, QKV_DA_V * cpb + h)),
            pl.BlockSpec((1, t, w), lambda bi, h, qi: (bi, qi, REST_DA_Z * cpb + h)),
            pl.BlockSpec((hp, 2, t, t), lambda bi, h, qi: (h, 0, 0, 0)),
            pl.BlockSpec((1, w), lambda bi, h, qi: (0, h)),
        ],
        out_specs=pl.BlockSpec((1, t, w), lambda bi, h, qi: (bi, qi, h)),
        out_shape=jax.ShapeDtypeStruct((b, s, BRANCH_WIDTH), BF16),
        scratch_shapes=[
            pltpu.VMEM((hp, s // t, hd, t), BF16),
            pltpu.VMEM((hp, s, 2 * hd), BF16),
            pltpu.VMEM((hp, 2, t, hd), BF16),
            pltpu.VMEM((hp, 2, t, LANES), F32),
            pltpu.VMEM((hp, 2, t, 2 * hd), F32),
        ],
        compiler_params=_params("parallel", "parallel", "arbitrary"),
        name="diff_attention",
    )(lamv, qkv, qkv, qkv, rest, bias, da_norm_g.reshape(1, BRANCH_WIDTH))


def _sb_kernel(q_ref, k_ref, v_ref, z_ref, o_ref, kt_scr, q_scr, c_scr, acc_scr, *, t, hp):
    qi = pl.program_id(2)
    nt = kt_scr.shape[1]
    hd = SB_HEAD_DIM
    cb = min(SB_CUMSUM_BLOCK, t)

    @pl.when(qi == 0)
    def _():
        for h in range(hp):
            for jj in range(nt):
                kt_scr[h, jj] = k_ref[0, jj * t:(jj + 1) * t, h * hd:(h + 1) * hd].astype(F32).T.astype(BF16)

    for h in range(hp):
        q_scr[h] = (q_ref[0, :, h * hd:(h + 1) * hd].astype(F32) * (SB_HEAD_DIM ** -0.5 * LOG2E)).astype(BF16)
    c_scr[...] = jnp.zeros(c_scr.shape, F32)
    acc_scr[...] = jnp.zeros(acc_scr.shape, F32)

    def tile(j, diag):
        off = pl.multiple_of(j * t, t)
        r2 = lax.broadcasted_iota(jnp.int32, (2 * cb, cb), 0)
        c2 = lax.broadcasted_iota(jnp.int32, (2 * cb, cb), 1)
        upper2 = jnp.where((r2 % cb) > c2, 1.0, 0.0).astype(BF16)
        if diag:
            row = lax.broadcasted_iota(jnp.int32, (t, t), 0)
            col = lax.broadcasted_iota(jnp.int32, (t, t), 1)
            causal = col < row
        for h in range(hp):
            v = v_ref[0, pl.ds(off, t), h * hd:(h + 1) * hd]
            z2 = _dot(q_scr[h], kt_scr[h, j])
            e2 = jnp.log(1.0 + jnp.exp2(_neg_abs(z2))) * (1.0 / LN2)
            lb2 = jnp.minimum(z2, 0.0) - e2
            l12 = lb2 - z2
            if diag:
                l12 = jnp.where(causal, l12, 0.0)
            c = c_scr[h]
            parts = []
            for blk in reversed(range(t // cb)):
                lblk = l12[:, blk * cb:(blk + 1) * cb]
                hi = lblk.astype(BF16)
                lo = (lblk - hi.astype(F32)).astype(BF16)
                cs = _dot(jnp.concatenate([hi, lo], axis=1), upper2)
                parts.append(_lane_tile(c, cb // LANES) + cs)
                c = c + jnp.broadcast_to(cs[:, 0:1] + lblk[:, 0:1], c.shape)
            c_scr[h] = c
            a = jnp.exp2(lb2 + jnp.concatenate(parts[::-1], axis=1))
            if diag:
                a = jnp.where(causal, a, 0.0)
            acc_scr[h] += _dot(a.astype(BF16), v)

    tile(qi, True)

    def body(jj, carry):
        tile(qi - 1 - jj, False)
        return carry

    lax.fori_loop(0, qi, body, 0)
    for h in range(hp):
        o_ref[0, :, h * hd:(h + 1) * hd] = (acc_scr[h] * _silu(z_ref[0, :, h * hd:(h + 1) * hd])).astype(o_ref.dtype)


def stick_breaking_branch(qkv, rest, t, hp):
    b, s, _ = qkv.shape
    t = _tile(s, t)
    hd = SB_HEAD_DIM
    w = hp * hd
    cpb = BRANCH_WIDTH // w
    return pl.pallas_call(
        functools.partial(_sb_kernel, t=t, hp=hp),
        grid=(b, SB_HEADS // hp, s // t),
        in_specs=[
            pl.BlockSpec((1, t, w), lambda bi, h, qi: (bi, qi, QKV_SB_Q * cpb + h)),
            pl.BlockSpec((1, s, w), lambda bi, h, qi: (bi, 0, QKV_SB_K * cpb + h)),
            pl.BlockSpec((1, s, w), lambda bi, h, qi: (bi, 0, QKV_SB_V * cpb + h)),
            pl.BlockSpec((1, t, w), lambda bi, h, qi: (bi, qi, REST_SB_Z * cpb + h)),
        ],
        out_specs=pl.BlockSpec((1, t, w), lambda bi, h, qi: (bi, qi, h)),
        out_shape=jax.ShapeDtypeStruct((b, s, BRANCH_WIDTH), BF16),
        scratch_shapes=[
            pltpu.VMEM((hp, s // t, hd, t), BF16),
            pltpu.VMEM((hp, t, hd), BF16),
            pltpu.VMEM((hp, t, LANES), F32),
            pltpu.VMEM((hp, t, hd), F32),
        ],
        compiler_params=_params("parallel", "parallel", "arbitrary"),
        name="stick_breaking",
    )(qkv, qkv, qkv, rest)


def _pool_kernel(u_ref, z_ref, w_ref, sc_ref, o_ref, ext_scr, *, tt):
    ti = pl.program_id(1)

    @pl.when(ti == 0)
    def _():
        ext_scr[0:POOL_HALO] = jnp.zeros((POOL_HALO, BRANCH_WIDTH), F32)

    ext_scr[POOL_HALO:POOL_HALO + tt] = u_ref[0]
    pos = ti * tt + lax.broadcasted_iota(jnp.int32, (tt, 1), 0)
    for g, w in enumerate(POOL_WINDOWS):
        cols = slice(g * POOL_GROUP, (g + 1) * POOL_GROUP)
        u = ext_scr[POOL_HALO:POOL_HALO + tt, cols]
        win = u
        for j in range(1, w):
            win = win + ext_scr[POOL_HALO - j:POOL_HALO - j + tt, cols]
        count = jnp.minimum(pos + 1, w).astype(F32)
        pooled = win / count - u
        mixed = _dot(pooled.astype(BF16), w_ref[g]) * sc_ref[:, cols]
        o_ref[0, :, cols] = (mixed * _silu(z_ref[0, :, cols])).astype(o_ref.dtype)
    ext_scr[0:POOL_HALO] = ext_scr[tt:tt + POOL_HALO]


def pool_branch(rest, w_pool_bf16, pool_scale, tt=512):
    b, s, _ = rest.shape
    tt = _tile(s, tt)
    assert tt >= POOL_HALO and max(POOL_WINDOWS) <= POOL_HALO
    return pl.pallas_call(
        functools.partial(_pool_kernel, tt=tt),
        grid=(b, s // tt),
        in_specs=[
            pl.BlockSpec((1, tt, BRANCH_WIDTH), lambda bi, ti: (bi, ti, REST_POOL_U)),
            pl.BlockSpec((1, tt, BRANCH_WIDTH), lambda bi, ti: (bi, ti, REST_POOL_Z)),
            pl.BlockSpec(w_pool_bf16.shape, lambda bi, ti: (0, 0, 0)),
            pl.BlockSpec((1, BRANCH_WIDTH), lambda bi, ti: (0, 0)),
        ],
        out_specs=pl.BlockSpec((1, tt, BRANCH_WIDTH), lambda bi, ti: (bi, ti, 0)),
        out_shape=jax.ShapeDtypeStruct((b, s, BRANCH_WIDTH), BF16),
        scratch_shapes=[pltpu.VMEM((tt + POOL_HALO, BRANCH_WIDTH), F32)],
        compiler_params=_params("parallel", "arbitrary"),
        name="multiscale_pool",
    )(rest, rest, w_pool_bf16, pool_scale.reshape(1, BRANCH_WIDTH))


def _mem_attn_kernel(q_ref, z_ref, mk_ref, mv_ref, o_ref):
    q = (q_ref[0] * (MEM_HEAD_DIM ** -0.5)).astype(BF16)
    s = _dot_nt(q, mk_ref[0].astype(BF16))
    p = jnp.exp(s - jnp.max(s, axis=-1, keepdims=True))
    l = jnp.sum(p, axis=-1, keepdims=True)
    o = _dot(p.astype(BF16), mv_ref[0].astype(BF16)) / l
    o_ref[0] = (o * _silu(z_ref[0])).astype(o_ref.dtype)


def memory_branch(qkv, rest, mkv, tq=512):
    b, s, _ = qkv.shape
    m = mkv.shape[1]
    tq = _tile(s, tq)
    hd = MEM_HEAD_DIM
    cpb = BRANCH_WIDTH // hd
    return pl.pallas_call(
        _mem_attn_kernel,
        grid=(b, MEM_HEADS, s // tq),
        in_specs=[
            pl.BlockSpec((1, tq, hd), lambda bi, h, qi: (bi, qi, QKV_MEM_Q * cpb + h)),
            pl.BlockSpec((1, tq, hd), lambda bi, h, qi: (bi, qi, REST_MEM_Z * cpb + h)),
            pl.BlockSpec((1, m, hd), lambda bi, h, qi: (bi, 0, h)),
            pl.BlockSpec((1, m, hd), lambda bi, h, qi: (bi, 0, cpb + h)),
        ],
        out_specs=pl.BlockSpec((1, tq, hd), lambda bi, h, qi: (bi, qi, h)),
        out_shape=jax.ShapeDtypeStruct((b, s, BRANCH_WIDTH), BF16),
        compiler_params=_params("parallel", "parallel", "parallel"),
        name="memory_attention",
    )(qkv, rest, mkv, mkv)


def _merge_kernel(b0_ref, b1_ref, b2_ref, b3_ref, g0_ref, g1_ref, g2_ref, g3_ref, gb_ref, w_ref, o_ref):
    branches = (b0_ref, b1_ref, b2_ref, b3_ref)
    gates = (g0_ref, g1_ref, g2_ref, g3_ref)
    merged = None
    for n in range(N_BRANCH):
        gate = 1.0 / (1.0 + jnp.exp(-(gates[n][...] + gb_ref[n:n + 1, :])))
        term = gate * _dot(branches[n][...], w_ref[n])
        merged = term if merged is None else merged + term
    o_ref[...] = merged.astype(o_ref.dtype)


def gated_merge(branches, rest2d, gate_b, w_branch_bf16, tm=512, tn=512):
    m = rest2d.shape[0]
    d = w_branch_bf16.shape[2]
    tm, tn = _tile(m, tm), _tile(d, tn)
    gate_col0 = REST_GATES * BRANCH_WIDTH // tn
    per_gate = d // tn
    br_spec = pl.BlockSpec((tm, BRANCH_WIDTH), lambda i, j: (i, 0))

    def gate_spec(n):
        return pl.BlockSpec((tm, tn), lambda i, j: (i, gate_col0 + n * per_gate + j))

    return pl.pallas_call(
        _merge_kernel,
        grid=(m // tm, d // tn),
        in_specs=[br_spec] * N_BRANCH + [gate_spec(n) for n in range(N_BRANCH)] + [
            pl.BlockSpec((N_BRANCH, tn), lambda i, j: (0, j)),
            pl.BlockSpec((N_BRANCH, BRANCH_WIDTH, tn), lambda i, j: (0, 0, j)),
        ],
        out_specs=pl.BlockSpec((tm, tn), lambda i, j: (i, j)),
        out_shape=jax.ShapeDtypeStruct((m, d), BF16),
        compiler_params=_params("parallel", "arbitrary"),
        name="gated_merge",
    )(*branches, rest2d, rest2d, rest2d, rest2d, gate_b, w_branch_bf16)


def _split_in_proj(w):
    cols = lambda sl: w[:, sl * BRANCH_WIDTH:(sl + 1) * BRANCH_WIDTH]
    w_qkv = jnp.concatenate([cols(sl) for sl in QKV_SLICES], axis=1)
    w_rest = jnp.concatenate([cols(sl) for sl in REST_SLICES] + [w[:, N_SLICES * BRANCH_WIDTH:]], axis=1)
    return w_qkv.astype(BF16), w_rest.astype(BF16)


def kernel(x, mem, rel_bias, norm_g, w_in, gate_b, lam_q1, lam_k1, lam_q2, lam_k2, da_norm_g, w_pool,
           pool_scale, mem_norm_g, w_mem_kv, w_branch, w_out, final_g):
    b, s, d = x.shape
    depth = norm_g.shape[0]
    n_mem = mem.shape[1]
    t = min(ATTN_TILE, s)
    hp = ATTN_HEADS_PER_STEP

    bias = bias_tiles(rel_bias, t)
    x2d = x.reshape(b * s, d)
    mem2d = mem.reshape(b * n_mem, d)
    for l in range(depth):
        lam_init = 0.8 - 0.6 * math.exp(-0.3 * l)
        w_qkv, w_rest = _split_in_proj(w_in[l])
        qkv = norm_matmul(x2d, norm_g[l], w_qkv, BF16).reshape(b, s, -1)
        rest2d = norm_matmul(x2d, norm_g[l], w_rest, F32)
        rest = rest2d.reshape(b, s, -1)

        lamv = jnp.stack([lam_q1[l], lam_k1[l], lam_q2[l], lam_k2[l]])
        br_da = diff_attention_branch(qkv, rest, bias, lamv, da_norm_g[l], lam_init, t, hp)
        br_sb = stick_breaking_branch(qkv, rest, t, hp)
        br_pool = pool_branch(rest, w_pool[l].astype(BF16), pool_scale[l])
        mkv = norm_matmul(mem2d, mem_norm_g[l], w_mem_kv[l].astype(BF16), F32)
        br_mem = memory_branch(qkv, rest, mkv.reshape(b, n_mem, 2 * BRANCH_WIDTH))

        branches = [br.reshape(b * s, BRANCH_WIDTH) for br in (br_da, br_sb, br_pool, br_mem)]
        merged = gated_merge(branches, rest2d, gate_b[l], w_branch[l].astype(BF16))
        x2d = matmul_residual(merged, w_out[l].astype(BF16), x2d)
    return rmsnorm(x2d, final_g, F32).reshape(b, s, d)
```

```python
import functools
import math

import jax
import jax.numpy as jnp
from jax import lax
from jax.experimental import pallas as pl
from jax.experimental.pallas import tpu as pltpu

F32 = jnp.float32
BF16 = jnp.bfloat16

BRANCH_WIDTH = 1024
N_BRANCH = 4
N_SLICES = 12
DA_HEADS = 8
DA_QK_DIM = 64
DA_V_DIM = 2 * DA_QK_DIM
SB_HEADS = 8
SB_HEAD_DIM = BRANCH_WIDTH // SB_HEADS
POOL_WINDOWS = (2, 4, 8, 16)
POOL_GROUP = BRANCH_WIDTH // len(POOL_WINDOWS)
MEM_HEADS = 4
MEM_HEAD_DIM = BRANCH_WIDTH // MEM_HEADS
REL_BUCKETS = 32
REL_MAX_DIST = 128
EPS = 1e-6

SL_DA_Q, SL_DA_K, SL_DA_V, SL_DA_Z = 0, 1, 2, 3
SL_SB_Q, SL_SB_K, SL_SB_V, SL_SB_Z = 4, 5, 6, 7
SL_POOL_U, SL_POOL_Z, SL_MEM_Q, SL_MEM_Z = 8, 9, 10, 11
QKV_SLICES = (SL_DA_Q, SL_DA_K, SL_DA_V, SL_SB_Q, SL_SB_K, SL_SB_V, SL_MEM_Q)
REST_SLICES = (SL_DA_Z, SL_SB_Z, SL_POOL_U, SL_POOL_Z, SL_MEM_Z)
QKV_DA_Q, QKV_DA_K, QKV_DA_V, QKV_SB_Q, QKV_SB_K, QKV_SB_V, QKV_MEM_Q = range(len(QKV_SLICES))
REST_DA_Z, REST_SB_Z, REST_POOL_U, REST_POOL_Z, REST_MEM_Z, REST_GATES = range(len(REST_SLICES) + 1)

V7X_VMEM_LIMIT_BYTES = 56 * 1024 * 1024
LANES = 128
POOL_HALO = 16
NEG_BIG = -1e30
ATTN_TILE = 512
ATTN_HEADS_PER_STEP = 2
SB_CUMSUM_BLOCK = 256
LOG2E = 1.4426950408889634
LN2 = 0.6931471805599453


def _params(*sem):
    return pltpu.CompilerParams(dimension_semantics=sem, vmem_limit_bytes=V7X_VMEM_LIMIT_BYTES)


def _tile(n, t):
    t = min(t, n)
    assert n % t == 0, (n, t)
    return t


def _silu(z):
    return z * (1.0 / (1.0 + jnp.exp(-z)))


def _dot_nt(a, b):
    return lax.dot_general(a, b, (((1,), (1,)), ((), ())), preferred_element_type=F32)


def _dot(a, b):
    return jnp.dot(a, b, preferred_element_type=F32)


def _lane_tile(x, n):
    return x if n == 1 else jnp.concatenate([x] * n, axis=1)


def _neg_abs(x):
    u = lax.bitcast_convert_type(x, jnp.uint32) | jnp.uint32(0x80000000)
    return lax.bitcast_convert_type(u, F32)


def _norm_matmul_kernel(x_ref, g_ref, w_ref, o_ref, h_scr):
    @pl.when(pl.program_id(1) == 0)
    def _():
        x = x_ref[...]
        y = x * lax.rsqrt(jnp.mean(x * x, axis=-1, keepdims=True) + EPS)
        h_scr[...] = (y * g_ref[...]).astype(BF16)

    o_ref[...] = _dot(h_scr[...], w_ref[...]).astype(o_ref.dtype)


def norm_matmul(x2d, g, w_bf16, out_dtype, tm=1024, tn=1024):
    m, d = x2d.shape
    n = w_bf16.shape[1]
    tm, tn = _tile(m, tm), _tile(n, tn)
    return pl.pallas_call(
        _norm_matmul_kernel,
        grid=(m // tm, n // tn),
        in_specs=[
            pl.BlockSpec((tm, d), lambda i, j: (i, 0)),
            pl.BlockSpec((1, d), lambda i, j: (0, 0)),
            pl.BlockSpec((d, tn), lambda i, j: (0, j)),
        ],
        out_specs=pl.BlockSpec((tm, tn), lambda i, j: (i, j)),
        out_shape=jax.ShapeDtypeStruct((m, n), out_dtype),
        scratch_shapes=[pltpu.VMEM((tm, d), BF16)],
        compiler_params=_params("parallel", "arbitrary"),
        name="norm_matmul",
    )(x2d, g.reshape(1, d), w_bf16)


def _matmul_residual_kernel(a_ref, w_ref, r_ref, g_ref, o_ref, *, final_norm):
    y = r_ref[...] + _dot(a_ref[...], w_ref[...])
    if final_norm:
        y = (y * lax.rsqrt(jnp.mean(y * y, axis=-1, keepdims=True) + EPS)) * g_ref[...]
    o_ref[...] = y


def matmul_residual(a_bf16, w_bf16, res, norm_g=None, tm=512):
    m, k = a_bf16.shape
    n = w_bf16.shape[1]
    tm = _tile(m, tm)
    g = jnp.ones((1, n), F32) if norm_g is None else norm_g.reshape(1, n)
    return pl.pallas_call(
        functools.partial(_matmul_residual_kernel, final_norm=norm_g is not None),
        grid=(m // tm,),
        in_specs=[
            pl.BlockSpec((tm, k), lambda i: (i, 0)),
            pl.BlockSpec((k, n), lambda i: (0, 0)),
            pl.BlockSpec((tm, n), lambda i: (i, 0)),
            pl.BlockSpec((1, n), lambda i: (0, 0)),
        ],
        out_specs=pl.BlockSpec((tm, n), lambda i: (i, 0)),
        out_shape=jax.ShapeDtypeStruct((m, n), F32),
        compiler_params=_params("parallel"),
        name="out_proj_residual",
    )(a_bf16, w_bf16, res, g)


def _bias_tiles_kernel(rb_ref, o_ref, *, t):
    h = pl.program_id(0)
    qi = lax.broadcasted_iota(jnp.int32, (t, t), 0)
    ki = lax.broadcasted_iota(jnp.int32, (t, t), 1)
    max_exact = REL_BUCKETS // 2
    far = rb_ref[REL_BUCKETS - 1, h]
    for sel in range(2):
        rel = qi - ki + sel * t
        n = jnp.maximum(rel, 0)
        nf = jnp.maximum(n, 1).astype(F32)
        large = max_exact + (jnp.log(nf / max_exact) / math.log(REL_MAX_DIST / max_exact)
                             * (REL_BUCKETS - max_exact)).astype(jnp.int32)
        large = jnp.minimum(large, REL_BUCKETS - 1)
        bucket = jnp.where(n < max_exact, n, large)
        val = jnp.zeros((t, t), F32)
        for b in range(REL_BUCKETS - 1):
            val = jnp.where(bucket == b, rb_ref[b, h] - far, val)
        if sel == 0:
            val = jnp.where(rel >= 0, val, NEG_BIG)
        o_ref[0, sel] = val


def bias_tiles(rel_bias, t):
    assert t >= REL_MAX_DIST
    heads = rel_bias.shape[1]
    return pl.pallas_call(
        functools.partial(_bias_tiles_kernel, t=t),
        grid=(heads,),
        in_specs=[pl.BlockSpec(memory_space=pltpu.SMEM)],
        out_specs=pl.BlockSpec((1, 2, t, t), lambda h: (h, 0, 0, 0)),
        out_shape=jax.ShapeDtypeStruct((heads, 2, t, t), F32),
        compiler_params=_params("parallel"),
        name="t5_bias_tiles",
    )(rel_bias)


def _da_kernel(lamv_ref, q_ref, k_ref, v_ref, z_ref, bias_ref, g_ref, o_ref,
               kt_scr, v_scr, q_scr, m_scr, acc_scr, *, t, hp, lam_init):
    qi = pl.program_id(2)
    nt = kt_scr.shape[1]
    hd = DA_V_DIM

    @pl.when(qi == 0)
    def _():
        for h in range(hp):
            for jj in range(nt):
                kt_scr[h, jj] = k_ref[0, jj * t:(jj + 1) * t, h * hd:(h + 1) * hd].astype(F32).T.astype(BF16)
            v_scr[h, :, 0:hd] = v_ref[0, :, h * hd:(h + 1) * hd]
            v_scr[h, :, hd:2 * hd] = jnp.ones((v_scr.shape[1], hd), BF16)

    lane = lax.broadcasted_iota(jnp.int32, (t, hd), 1)
    for h in range(hp):
        q = q_ref[0, :, h * hd:(h + 1) * hd] * (DA_QK_DIM ** -0.5)
        q_scr[h, 0] = jnp.where(lane < DA_QK_DIM, q, 0.0).astype(BF16)
        q_scr[h, 1] = jnp.where(lane >= DA_QK_DIM, q, 0.0).astype(BF16)
    m_scr[...] = jnp.full(m_scr.shape, -jnp.inf, F32)
    acc_scr[...] = jnp.zeros(acc_scr.shape, F32)

    def tile(j, bias_sel):
        off = pl.multiple_of(j * t, t)
        for h in range(hp):
            kt = kt_scr[h, j]
            v = v_scr[h, pl.ds(off, t), :]
            for c in range(2):
                s = _dot(q_scr[h, c], kt)
                if bias_sel is not None:
                    s = s + bias_ref[h, bias_sel]
                m_prev = m_scr[h, c]
                m_new = jnp.maximum(m_prev, jnp.max(s, axis=-1, keepdims=True))
                alpha = jnp.exp(m_prev - m_new)
                p = jnp.exp(s - _lane_tile(m_new, t // LANES))
                acc_scr[h, c] = _lane_tile(alpha, 2) * acc_scr[h, c] + _dot(p.astype(BF16), v)
                m_scr[h, c] = m_new

    n_far = jnp.maximum(qi - 1, 0)

    def far_body(jp, carry):
        tile(2 * jp, None)
        tile(2 * jp + 1, None)
        return carry

    lax.fori_loop(0, n_far // 2, far_body, 0)

    @pl.when(n_far % 2 == 1)
    def _():
        tile(n_far - 1, None)

    @pl.when(qi >= 1)
    def _():
        tile(qi - 1, 1)
        tile(qi, 0)

    @pl.when(qi == 0)
    def _():
        tile(0, 0)

    lv = lamv_ref[...]
    s1 = jnp.sum(lv[0:1] * lv[1:2], axis=-1, keepdims=True)
    s2 = jnp.sum(lv[2:3] * lv[3:4], axis=-1, keepdims=True)
    lam = jnp.exp(s1) - jnp.exp(s2) + lam_init
    for h in range(hp):
        a0 = acc_scr[h, 0]
        a1 = acc_scr[h, 1]
        o = a0[:, 0:hd] / a0[:, hd:2 * hd] - lam * (a1[:, 0:hd] / a1[:, hd:2 * hd])
        y = o * lax.rsqrt(jnp.mean(o * o, axis=-1, keepdims=True) + EPS)
        y = (y * g_ref[:, h * hd:(h + 1) * hd]) * (1.0 - lam_init)
        o_ref[0, :, h * hd:(h + 1) * hd] = (y * _silu(z_ref[0, :, h * hd:(h + 1) * hd])).astype(o_ref.dtype)


def diff_attention_branch(qkv, rest, bias, lamv, da_norm_g, lam_init, t, hp):
    b, s, _ = qkv.shape
    t = _tile(s, t)
    hd = DA_V_DIM
    w = hp * hd
    cpb = BRANCH_WIDTH // w
    return pl.pallas_call(
        functools.partial(_da_kernel, t=t, hp=hp, lam_init=lam_init),
        grid=(b, DA_HEADS // hp, s // t),
        in_specs=[
            pl.BlockSpec((4, DA_QK_DIM), lambda bi, h, qi: (0, 0)),
            pl.BlockSpec((1, t, w), lambda bi, h, qi: (bi, qi, QKV_DA_Q * cpb + h)),
            pl.BlockSpec((1, s, w), lambda bi, h, qi: (bi, 0, QKV_DA_K * cpb + h)),
            pl.BlockSpec((1, s, w), lambda bi, h, qi: (bi, 0, QKV_DA_V * cpb + h)),
            pl.BlockSpec((1, t, w), lambda bi, h, qi: (bi, qi, REST_DA_Z * cpb + h)),
            pl.BlockSpec((hp, 2, t, t), lambda bi, h, qi: (h, 0, 0, 0)),
            pl.BlockSpec((1, w), lambda bi, h, qi: (0, h)),
        ],
        out_specs=pl.BlockSpec((1, t, w), lambda bi, h, qi: (bi, qi, h)),
        out_shape=jax.ShapeDtypeStruct((b, s, BRANCH_WIDTH), BF16),
        scratch_shapes=[
            pltpu.VMEM((hp, s // t, hd, t), BF16),
            pltpu.VMEM((hp, s, 2 * hd), BF16),
            pltpu.VMEM((hp, 2, t, hd), BF16),
            pltpu.VMEM((hp, 2, t, LANES), F32),
            pltpu.VMEM((hp, 2, t, 2 * hd), F32),
        ],
        compiler_params=_params("parallel", "parallel", "arbitrary"),
        name="diff_attention",
    )(lamv, qkv, qkv, qkv, rest, bias, da_norm_g.reshape(1, BRANCH_WIDTH))


def _sb_kernel(q_ref, k_ref, v_ref, z_ref, o_ref, kt_scr, q_scr, c_scr, acc_scr, *, t, hp):
    qi = pl.program_id(2)
    nt = kt_scr.shape[1]
    hd = SB_HEAD_DIM
    cb = min(SB_CUMSUM_BLOCK, t)

    @pl.when(qi == 0)
    def _():
        for h in range(hp):
            for jj in range(nt):
                kt_scr[h, jj] = k_ref[0, jj * t:(jj + 1) * t, h * hd:(h + 1) * hd].astype(F32).T.astype(BF16)

    for h in range(hp):
        q_scr[h] = (q_ref[0, :, h * hd:(h + 1) * hd].astype(F32) * (SB_HEAD_DIM ** -0.5 * LOG2E)).astype(BF16)
    c_scr[...] = jnp.zeros(c_scr.shape, F32)
    acc_scr[...] = jnp.zeros(acc_scr.shape, F32)

    def tile(j, diag):
        off = pl.multiple_of(j * t, t)
        r2 = lax.broadcasted_iota(jnp.int32, (cb, cb), 0)
        c2 = lax.broadcasted_iota(jnp.int32, (cb, cb), 1)
        upper = jnp.where(r2 >= c2, 1.0, 0.0).astype(BF16)
        if diag:
            row = lax.broadcasted_iota(jnp.int32, (t, t), 0)
            col = lax.broadcasted_iota(jnp.int32, (t, t), 1)
            causal = col < row
        for h in range(hp):
            v = v_ref[0, pl.ds(off, t), h * hd:(h + 1) * hd]
            z2 = _dot(q_scr[h], kt_scr[h, j])
            w = jnp.maximum(z2, 0.0) + jnp.log(1.0 + jnp.exp2(_neg_abs(z2))) * (1.0 / LN2)
            if diag:
                w = jnp.where(causal, w, 0.0)
            c = c_scr[h]
            parts = []
            for blk in reversed(range(t // cb)):
                sl = slice(blk * cb, (blk + 1) * cb)
                cs = _dot(w[:, sl].astype(BF16), upper)
                parts.append((z2[:, sl] - _lane_tile(c, cb // LANES)) - cs)
                c = c + jnp.broadcast_to(cs[:, 0:1], c.shape)
            c_scr[h] = c
            a = jnp.exp2(jnp.concatenate(parts[::-1], axis=1))
            if diag:
                a = jnp.where(causal, a, 0.0)
            acc_scr[h] += _dot(a.astype(BF16), v)

    @pl.when(qi == 0)
    def _():
        tile(0, True)

    @pl.when(qi >= 1)
    def _():
        tile(qi, True)
        tile(qi - 1, False)

    n_rest = jnp.maximum(qi - 1, 0)

    def body(jp, carry):
        tile(qi - 2 - 2 * jp, False)
        tile(qi - 3 - 2 * jp, False)
        return carry

    lax.fori_loop(0, n_rest // 2, body, 0)

    @pl.when(n_rest % 2 == 1)
    def _():
        tile(0, False)

    for h in range(hp):
        o_ref[0, :, h * hd:(h + 1) * hd] = (acc_scr[h] * _silu(z_ref[0, :, h * hd:(h + 1) * hd])).astype(o_ref.dtype)


def stick_breaking_branch(qkv, rest, t, hp):
    b, s, _ = qkv.shape
    t = _tile(s, t)
    hd = SB_HEAD_DIM
    w = hp * hd
    cpb = BRANCH_WIDTH // w
    return pl.pallas_call(
        functools.partial(_sb_kernel, t=t, hp=hp),
        grid=(b, SB_HEADS // hp, s // t),
        in_specs=[
            pl.BlockSpec((1, t, w), lambda bi, h, qi: (bi, qi, QKV_SB_Q * cpb + h)),
            pl.BlockSpec((1, s, w), lambda bi, h, qi: (bi, 0, QKV_SB_K * cpb + h)),
            pl.BlockSpec((1, s, w), lambda bi, h, qi: (bi, 0, QKV_SB_V * cpb + h)),
            pl.BlockSpec((1, t, w), lambda bi, h, qi: (bi, qi, REST_SB_Z * cpb + h)),
        ],
        out_specs=pl.BlockSpec((1, t, w), lambda bi, h, qi: (bi, qi, h)),
        out_shape=jax.ShapeDtypeStruct((b, s, BRANCH_WIDTH), BF16),
        scratch_shapes=[
            pltpu.VMEM((hp, s // t, hd, t), BF16),
            pltpu.VMEM((hp, t, hd), BF16),
            pltpu.VMEM((hp, t, LANES), F32),
            pltpu.VMEM((hp, t, hd), F32),
        ],
        compiler_params=_params("parallel", "parallel", "arbitrary"),
        name="stick_breaking",
    )(qkv, qkv, qkv, rest)


def _pool_kernel(u_ref, z_ref, w_ref, sc_ref, o_ref, ext_scr, *, tt):
    ti = pl.program_id(1)

    @pl.when(ti == 0)
    def _():
        ext_scr[0:POOL_HALO] = jnp.zeros((POOL_HALO, BRANCH_WIDTH), F32)

    ext_scr[POOL_HALO:POOL_HALO + tt] = u_ref[0]
    pos = ti * tt + lax.broadcasted_iota(jnp.int32, (tt, 1), 0)
    for g, w in enumerate(POOL_WINDOWS):
        cols = slice(g * POOL_GROUP, (g + 1) * POOL_GROUP)
        u = ext_scr[POOL_HALO:POOL_HALO + tt, cols]
        win = u
        for j in range(1, w):
            win = win + ext_scr[POOL_HALO - j:POOL_HALO - j + tt, cols]
        count = jnp.minimum(pos + 1, w).astype(F32)
        pooled = win / count - u
        mixed = _dot(pooled.astype(BF16), w_ref[g]) * sc_ref[:, cols]
        o_ref[0, :, cols] = (mixed * _silu(z_ref[0, :, cols])).astype(o_ref.dtype)
    ext_scr[0:POOL_HALO] = ext_scr[tt:tt + POOL_HALO]


def pool_branch(rest, w_pool_bf16, pool_scale, tt=512):
    b, s, _ = rest.shape
    tt = _tile(s, tt)
    assert tt >= POOL_HALO and max(POOL_WINDOWS) <= POOL_HALO
    return pl.pallas_call(
        functools.partial(_pool_kernel, tt=tt),
        grid=(b, s // tt),
        in_specs=[
            pl.BlockSpec((1, tt, BRANCH_WIDTH), lambda bi, ti: (bi, ti, REST_POOL_U)),
            pl.BlockSpec((1, tt, BRANCH_WIDTH), lambda bi, ti: (bi, ti, REST_POOL_Z)),
            pl.BlockSpec(w_pool_bf16.shape, lambda bi, ti: (0, 0, 0)),
            pl.BlockSpec((1, BRANCH_WIDTH), lambda bi, ti: (0, 0)),
        ],
        out_specs=pl.BlockSpec((1, tt, BRANCH_WIDTH), lambda bi, ti: (bi, ti, 0)),
        out_shape=jax.ShapeDtypeStruct((b, s, BRANCH_WIDTH), BF16),
        scratch_shapes=[pltpu.VMEM((tt + POOL_HALO, BRANCH_WIDTH), F32)],
        compiler_params=_params("parallel", "arbitrary"),
        name="multiscale_pool",
    )(rest, rest, w_pool_bf16, pool_scale.reshape(1, BRANCH_WIDTH))


def _mem_attn_kernel(q_ref, z_ref, mk_ref, mv_ref, o_ref):
    q = (q_ref[0] * (MEM_HEAD_DIM ** -0.5)).astype(BF16)
    s = _dot_nt(q, mk_ref[0].astype(BF16))
    p = jnp.exp(s - jnp.max(s, axis=-1, keepdims=True))
    l = jnp.sum(p, axis=-1, keepdims=True)
    o = _dot(p.astype(BF16), mv_ref[0].astype(BF16)) / l
    o_ref[0] = (o * _silu(z_ref[0])).astype(o_ref.dtype)


def memory_branch(qkv, rest, mkv, tq=512):
    b, s, _ = qkv.shape
    m = mkv.shape[1]
    tq = _tile(s, tq)
    hd = MEM_HEAD_DIM
    cpb = BRANCH_WIDTH // hd
    return pl.pallas_call(
        _mem_attn_kernel,
        grid=(b, MEM_HEADS, s // tq),
        in_specs=[
            pl.BlockSpec((1, tq, hd), lambda bi, h, qi: (bi, qi, QKV_MEM_Q * cpb + h)),
            pl.BlockSpec((1, tq, hd), lambda bi, h, qi: (bi, qi, REST_MEM_Z * cpb + h)),
            pl.BlockSpec((1, m, hd), lambda bi, h, qi: (bi, 0, h)),
            pl.BlockSpec((1, m, hd), lambda bi, h, qi: (bi, 0, cpb + h)),
        ],
        out_specs=pl.BlockSpec((1, tq, hd), lambda bi, h, qi: (bi, qi, h)),
        out_shape=jax.ShapeDtypeStruct((b, s, BRANCH_WIDTH), BF16),
        compiler_params=_params("parallel", "parallel", "parallel"),
        name="memory_attention",
    )(qkv, rest, mkv, mkv)


def _merge_kernel(b0_ref, b1_ref, b2_ref, b3_ref, g0_ref, g1_ref, g2_ref, g3_ref, gb_ref, w_ref, o_ref):
    branches = (b0_ref, b1_ref, b2_ref, b3_ref)
    gates = (g0_ref, g1_ref, g2_ref, g3_ref)
    merged = None
    for n in range(N_BRANCH):
        gate = 1.0 / (1.0 + jnp.exp(-(gates[n][...] + gb_ref[n:n + 1, :])))
        term = gate * _dot(branches[n][...], w_ref[n])
        merged = term if merged is None else merged + term
    o_ref[...] = merged.astype(o_ref.dtype)


def gated_merge(branches, rest2d, gate_b, w_branch_bf16, tm=512, tn=512):
    m = rest2d.shape[0]
    d = w_branch_bf16.shape[2]
    tm, tn = _tile(m, tm), _tile(d, tn)
    gate_col0 = REST_GATES * BRANCH_WIDTH // tn
    per_gate = d // tn
    br_spec = pl.BlockSpec((tm, BRANCH_WIDTH), lambda i, j: (i, 0))

    def gate_spec(n):
        return pl.BlockSpec((tm, tn), lambda i, j: (i, gate_col0 + n * per_gate + j))

    return pl.pallas_call(
        _merge_kernel,
        grid=(m // tm, d // tn),
        in_specs=[br_spec] * N_BRANCH + [gate_spec(n) for n in range(N_BRANCH)] + [
            pl.BlockSpec((N_BRANCH, tn), lambda i, j: (0, j)),
            pl.BlockSpec((N_BRANCH, BRANCH_WIDTH, tn), lambda i, j: (0, 0, j)),
        ],
        out_specs=pl.BlockSpec((tm, tn), lambda i, j: (i, j)),
        out_shape=jax.ShapeDtypeStruct((m, d), BF16),
        compiler_params=_params("parallel", "arbitrary"),
        name="gated_merge",
    )(*branches, rest2d, rest2d, rest2d, rest2d, gate_b, w_branch_bf16)


def _split_in_proj(w):
    cols = lambda sl: w[:, sl * BRANCH_WIDTH:(sl + 1) * BRANCH_WIDTH]
    w_qkv = jnp.concatenate([cols(sl) for sl in QKV_SLICES], axis=1)
    w_rest = jnp.concatenate([cols(sl) for sl in REST_SLICES] + [w[:, N_SLICES * BRANCH_WIDTH:]], axis=1)
    return w_qkv.astype(BF16), w_rest.astype(BF16)


def kernel(x, mem, rel_bias, norm_g, w_in, gate_b, lam_q1, lam_k1, lam_q2, lam_k2, da_norm_g, w_pool,
           pool_scale, mem_norm_g, w_mem_kv, w_branch, w_out, final_g):
    b, s, d = x.shape
    depth = norm_g.shape[0]
    n_mem = mem.shape[1]
    t = min(ATTN_TILE, s)
    hp = ATTN_HEADS_PER_STEP

    bias = bias_tiles(rel_bias, t)
    x2d = x.reshape(b * s, d)
    mem2d = mem.reshape(b * n_mem, d)
    for l in range(depth):
        lam_init = 0.8 - 0.6 * math.exp(-0.3 * l)
        w_qkv, w_rest = _split_in_proj(w_in[l])
        qkv = norm_matmul(x2d, norm_g[l], w_qkv, BF16).reshape(b, s, -1)
        rest2d = norm_matmul(x2d, norm_g[l], w_rest, F32)
        rest = rest2d.reshape(b, s, -1)

        lamv = jnp.stack([lam_q1[l], lam_k1[l], lam_q2[l], lam_k2[l]])
        br_da = diff_attention_branch(qkv, rest, bias, lamv, da_norm_g[l], lam_init, t, hp)
        br_sb = stick_breaking_branch(qkv, rest, t, hp)
        br_pool = pool_branch(rest, w_pool[l].astype(BF16), pool_scale[l])
        mkv = norm_matmul(mem2d, mem_norm_g[l], w_mem_kv[l].astype(BF16), F32)
        br_mem = memory_branch(qkv, rest, mkv.reshape(b, n_mem, 2 * BRANCH_WIDTH))

        branches = [br.reshape(b * s, BRANCH_WIDTH) for br in (br_da, br_sb, br_pool, br_mem)]
        merged = gated_merge(branches, rest2d, gate_b[l], w_branch[l].astype(BF16))
        x2d = matmul_residual(merged, w_out[l].astype(BF16), x2d, final_g if l == depth - 1 else None)
    return x2d.reshape(b, s, d)
```

```python
import functools
import math

import jax
import jax.numpy as jnp
from jax import lax
from jax.experimental import pallas as pl
from jax.experimental.pallas import tpu as pltpu

F32 = jnp.float32
BF16 = jnp.bfloat16

BRANCH_WIDTH = 1024
N_BRANCH = 4
N_SLICES = 12
DA_HEADS = 8
DA_QK_DIM = 64
DA_V_DIM = 2 * DA_QK_DIM
SB_HEADS = 8
SB_HEAD_DIM = BRANCH_WIDTH // SB_HEADS
POOL_WINDOWS = (2, 4, 8, 16)
POOL_GROUP = BRANCH_WIDTH // len(POOL_WINDOWS)
MEM_HEADS = 4
MEM_HEAD_DIM = BRANCH_WIDTH // MEM_HEADS
REL_BUCKETS = 32
REL_MAX_DIST = 128
EPS = 1e-6

SL_DA_Q, SL_DA_K, SL_DA_V, SL_DA_Z = 0, 1, 2, 3
SL_SB_Q, SL_SB_K, SL_SB_V, SL_SB_Z = 4, 5, 6, 7
SL_POOL_U, SL_POOL_Z, SL_MEM_Q, SL_MEM_Z = 8, 9, 10, 11
QKV_SLICES = (SL_DA_Q, SL_DA_K, SL_DA_V, SL_SB_Q, SL_SB_K, SL_SB_V, SL_MEM_Q)
REST_SLICES = (SL_DA_Z, SL_SB_Z, SL_POOL_U, SL_POOL_Z, SL_MEM_Z)
QKV_DA_Q, QKV_DA_K, QKV_DA_V, QKV_SB_Q, QKV_SB_K, QKV_SB_V, QKV_MEM_Q = range(len(QKV_SLICES))
REST_DA_Z, REST_SB_Z, REST_POOL_U, REST_POOL_Z, REST_MEM_Z, REST_GATES = range(len(REST_SLICES) + 1)

V7X_VMEM_LIMIT_BYTES = 56 * 1024 * 1024
LANES = 128
POOL_HALO = 16
NEG_BIG = -1e30
ATTN_TILE = 512
ATTN_HEADS_PER_STEP = 2
SB_CUMSUM_BLOCK = 256
LOG2E = 1.4426950408889634
LN2 = 0.6931471805599453


def _params(*sem):
    return pltpu.CompilerParams(dimension_semantics=sem, vmem_limit_bytes=V7X_VMEM_LIMIT_BYTES)


def _tile(n, t):
    t = min(t, n)
    assert n % t == 0, (n, t)
    return t


def _silu(z):
    return z * (1.0 / (1.0 + jnp.exp(-z)))


def _dot_nt(a, b):
    return lax.dot_general(a, b, (((1,), (1,)), ((), ())), preferred_element_type=F32)


def _dot(a, b):
    return jnp.dot(a, b, preferred_element_type=F32)


def _lane_tile(x, n):
    return x if n == 1 else jnp.concatenate([x] * n, axis=1)


def _neg_abs(x):
    u = lax.bitcast_convert_type(x, jnp.uint32) | jnp.uint32(0x80000000)
    return lax.bitcast_convert_type(u, F32)


def _norm_matmul_kernel(x_ref, g_ref, w_ref, o_ref, h_scr):
    @pl.when(pl.program_id(1) == 0)
    def _():
        x = x_ref[...]
        y = x * lax.rsqrt(jnp.mean(x * x, axis=-1, keepdims=True) + EPS)
        h_scr[...] = (y * g_ref[...]).astype(BF16)

    o_ref[...] = _dot(h_scr[...], w_ref[...]).astype(o_ref.dtype)


def norm_matmul(x2d, g, w_bf16, out_dtype, tm=1024, tn=1024):
    m, d = x2d.shape
    n = w_bf16.shape[1]
    tm, tn = _tile(m, tm), _tile(n, tn)
    return pl.pallas_call(
        _norm_matmul_kernel,
        grid=(m // tm, n // tn),
        in_specs=[
            pl.BlockSpec((tm, d), lambda i, j: (i, 0)),
            pl.BlockSpec((1, d), lambda i, j: (0, 0)),
            pl.BlockSpec((d, tn), lambda i, j: (0, j)),
        ],
        out_specs=pl.BlockSpec((tm, tn), lambda i, j: (i, j)),
        out_shape=jax.ShapeDtypeStruct((m, n), out_dtype),
        scratch_shapes=[pltpu.VMEM((tm, d), BF16)],
        compiler_params=_params("parallel", "arbitrary"),
        name="norm_matmul",
    )(x2d, g.reshape(1, d), w_bf16)


def _matmul_residual_kernel(a_ref, w_ref, r_ref, g_ref, o_ref, *, final_norm):
    y = r_ref[...] + _dot(a_ref[...], w_ref[...])
    if final_norm:
        y = (y * lax.rsqrt(jnp.mean(y * y, axis=-1, keepdims=True) + EPS)) * g_ref[...]
    o_ref[...] = y


def matmul_residual(a_bf16, w_bf16, res, norm_g=None, tm=512):
    m, k = a_bf16.shape
    n = w_bf16.shape[1]
    tm = _tile(m, tm)
    g = jnp.ones((1, n), F32) if norm_g is None else norm_g.reshape(1, n)
    return pl.pallas_call(
        functools.partial(_matmul_residual_kernel, final_norm=norm_g is not None),
        grid=(m // tm,),
        in_specs=[
            pl.BlockSpec((tm, k), lambda i: (i, 0)),
            pl.BlockSpec((k, n), lambda i: (0, 0)),
            pl.BlockSpec((tm, n), lambda i: (i, 0)),
            pl.BlockSpec((1, n), lambda i: (0, 0)),
        ],
        out_specs=pl.BlockSpec((tm, n), lambda i: (i, 0)),
        out_shape=jax.ShapeDtypeStruct((m, n), F32),
        compiler_params=_params("parallel"),
        name="out_proj_residual",
    )(a_bf16, w_bf16, res, g)


def _bias_tiles_kernel(rb_ref, o_ref, *, t):
    h = pl.program_id(0)
    qi = lax.broadcasted_iota(jnp.int32, (t, t), 0)
    ki = lax.broadcasted_iota(jnp.int32, (t, t), 1)
    max_exact = REL_BUCKETS // 2
    far = rb_ref[REL_BUCKETS - 1, h]
    for sel in range(2):
        rel = qi - ki + sel * t
        n = jnp.maximum(rel, 0)
        nf = jnp.maximum(n, 1).astype(F32)
        large = max_exact + (jnp.log(nf / max_exact) / math.log(REL_MAX_DIST / max_exact)
                             * (REL_BUCKETS - max_exact)).astype(jnp.int32)
        large = jnp.minimum(large, REL_BUCKETS - 1)
        bucket = jnp.where(n < max_exact, n, large)
        val = jnp.zeros((t, t), F32)
        for b in range(REL_BUCKETS - 1):
            val = jnp.where(bucket == b, rb_ref[b, h] - far, val)
        if sel == 0:
            val = jnp.where(rel >= 0, val, NEG_BIG)
        o_ref[0, sel] = val


def bias_tiles(rel_bias, t):
    assert t >= REL_MAX_DIST
    heads = rel_bias.shape[1]
    return pl.pallas_call(
        functools.partial(_bias_tiles_kernel, t=t),
        grid=(heads,),
        in_specs=[pl.BlockSpec(memory_space=pltpu.SMEM)],
        out_specs=pl.BlockSpec((1, 2, t, t), lambda h: (h, 0, 0, 0)),
        out_shape=jax.ShapeDtypeStruct((heads, 2, t, t), F32),
        compiler_params=_params("parallel"),
        name="t5_bias_tiles",
    )(rel_bias)


def _da_kernel(lamv_ref, q_ref, k_ref, v_ref, z_ref, bias_ref, g_ref, o_ref,
               kt_scr, v_scr, q_scr, m_scr, acc_scr, *, t, hp, lam_init):
    qi = pl.program_id(2)
    nt = kt_scr.shape[1]
    hd = DA_V_DIM

    @pl.when(qi == 0)
    def _():
        for h in range(hp):
            for jj in range(nt):
                kt_scr[h, jj] = k_ref[0, jj * t:(jj + 1) * t, h * hd:(h + 1) * hd].astype(F32).T.astype(BF16)
            v_scr[h, :, 0:hd] = v_ref[0, :, h * hd:(h + 1) * hd]
            v_scr[h, :, hd:2 * hd] = jnp.ones((v_scr.shape[1], hd), BF16)

    lane = lax.broadcasted_iota(jnp.int32, (t, hd), 1)
    for h in range(hp):
        q = q_ref[0, :, h * hd:(h + 1) * hd] * (DA_QK_DIM ** -0.5)
        q_scr[h, 0] = jnp.where(lane < DA_QK_DIM, q, 0.0).astype(BF16)
        q_scr[h, 1] = jnp.where(lane >= DA_QK_DIM, q, 0.0).astype(BF16)
    m_scr[...] = jnp.full(m_scr.shape, -jnp.inf, F32)
    acc_scr[...] = jnp.zeros(acc_scr.shape, F32)

    def tile(j, bias_sel):
        off = pl.multiple_of(j * t, t)
        for h in range(hp):
            kt = kt_scr[h, j]
            v = v_scr[h, pl.ds(off, t), :]
            for c in range(2):
                s = _dot(q_scr[h, c], kt)
                if bias_sel is not None:
                    s = s + bias_ref[h, bias_sel]
                m_prev = m_scr[h, c]
                m_new = jnp.maximum(m_prev, jnp.max(s, axis=-1, keepdims=True))
                alpha = jnp.exp(m_prev - m_new)
                p = jnp.exp(s - _lane_tile(m_new, t // LANES))
                acc_scr[h, c] = _lane_tile(alpha, 2) * acc_scr[h, c] + _dot(p.astype(BF16), v)
                m_scr[h, c] = m_new

    n_far = jnp.maximum(qi - 1, 0)

    def far_body(jp, carry):
        tile(2 * jp, None)
        tile(2 * jp + 1, None)
        return carry

    lax.fori_loop(0, n_far // 2, far_body, 0)

    @pl.when(n_far % 2 == 1)
    def _():
        tile(n_far - 1, None)

    @pl.when(qi >= 1)
    def _():
        tile(qi - 1, 1)
        tile(qi, 0)

    @pl.when(qi == 0)
    def _():
        tile(0, 0)

    lv = lamv_ref[...]
    s1 = jnp.sum(lv[0:1] * lv[1:2], axis=-1, keepdims=True)
    s2 = jnp.sum(lv[2:3] * lv[3:4], axis=-1, keepdims=True)
    lam = jnp.exp(s1) - jnp.exp(s2) + lam_init
    for h in range(hp):
        a0 = acc_scr[h, 0]
        a1 = acc_scr[h, 1]
        o = a0[:, 0:hd] / a0[:, hd:2 * hd] - lam * (a1[:, 0:hd] / a1[:, hd:2 * hd])
        y = o * lax.rsqrt(jnp.mean(o * o, axis=-1, keepdims=True) + EPS)
        y = (y * g_ref[:, h * hd:(h + 1) * hd]) * (1.0 - lam_init)
        o_ref[0, :, h * hd:(h + 1) * hd] = (y * _silu(z_ref[0, :, h * hd:(h + 1) * hd])).astype(o_ref.dtype)


def diff_attention_branch(qkv, rest, bias, lamv, da_norm_g, lam_init, t, hp):
    b, s, _ = qkv.shape
    t = _tile(s, t)
    hd = DA_V_DIM
    w = hp * hd
    cpb = BRANCH_WIDTH // w
    return pl.pallas_call(
        functools.partial(_da_kernel, t=t, hp=hp, lam_init=lam_init),
        grid=(b, DA_HEADS // hp, s // t),
        in_specs=[
            pl.BlockSpec((4, DA_QK_DIM), lambda bi, h, qi: (0, 0)),
            pl.BlockSpec((1, t, w), lambda bi, h, qi: (bi, qi, QKV_DA_Q * cpb + h)),
            pl.BlockSpec((1, s, w), lambda bi, h, qi: (bi, 0, QKV_DA_K * cpb + h)),
            pl.BlockSpec((1, s, w), lambda bi, h, qi: (bi, 0, QKV_DA_V * cpb + h)),
            pl.BlockSpec((1, t, w), lambda bi, h, qi: (bi, qi, REST_DA_Z * cpb + h)),
            pl.BlockSpec((hp, 2, t, t), lambda bi, h, qi: (h, 0, 0, 0)),
            pl.BlockSpec((1, w), lambda bi, h, qi: (0, h)),
        ],
        out_specs=pl.BlockSpec((1, t, w), lambda bi, h, qi: (bi, qi, h)),
        out_shape=jax.ShapeDtypeStruct((b, s, BRANCH_WIDTH), BF16),
        scratch_shapes=[
            pltpu.VMEM((hp, s // t, hd, t), BF16),
            pltpu.VMEM((hp, s, 2 * hd), BF16),
            pltpu.VMEM((hp, 2, t, hd), BF16),
            pltpu.VMEM((hp, 2, t, LANES), F32),
            pltpu.VMEM((hp, 2, t, 2 * hd), F32),
        ],
        compiler_params=_params("parallel", "parallel", "arbitrary"),
        name="diff_attention",
    )(lamv, qkv, qkv, qkv, rest, bias, da_norm_g.reshape(1, BRANCH_WIDTH))


def _sb_kernel(q_ref, k_ref, v_ref, z_ref, o_ref, kt_scr, q_scr, c_scr, acc_scr, *, t, hp):
    qi = pl.program_id(2)
    nt = kt_scr.shape[1]
    hd = SB_HEAD_DIM
    cb = min(SB_CUMSUM_BLOCK, t)

    @pl.when(qi == 0)
    def _():
        for h in range(hp):
            for jj in range(nt):
                kt_scr[h, jj] = k_ref[0, jj * t:(jj + 1) * t, h * hd:(h + 1) * hd].astype(F32).T.astype(BF16)

    for h in range(hp):
        q_scr[h] = (q_ref[0, :, h * hd:(h + 1) * hd].astype(F32) * (SB_HEAD_DIM ** -0.5 * LOG2E)).astype(BF16)
    c_scr[...] = jnp.zeros(c_scr.shape, F32)
    acc_scr[...] = jnp.zeros(acc_scr.shape, F32)

    def tile(j, diag):
        off = pl.multiple_of(j * t, t)
        r2 = lax.broadcasted_iota(jnp.int32, (cb, cb), 0)
        c2 = lax.broadcasted_iota(jnp.int32, (cb, cb), 1)
        upper = jnp.where(r2 >= c2, 1.0, 0.0).astype(BF16)
        if diag:
            row = lax.broadcasted_iota(jnp.int32, (t, t), 0)
            col = lax.broadcasted_iota(jnp.int32, (t, t), 1)
            causal = col < row
        for h in range(hp):
            v = v_ref[0, pl.ds(off, t), h * hd:(h + 1) * hd]
            z2 = _dot(q_scr[h], kt_scr[h, j])
            w = jnp.maximum(z2, 0.0) + jnp.log(1.0 + jnp.exp2(_neg_abs(z2))) * (1.0 / LN2)
            if diag:
                w = jnp.where(causal, w, 0.0)
            c = c_scr[h]
            parts = []
            for blk in reversed(range(t // cb)):
                sl = slice(blk * cb, (blk + 1) * cb)
                cs = _dot(w[:, sl].astype(BF16), upper)
                parts.append((z2[:, sl] - _lane_tile(c, cb // LANES)) - cs)
                c = c + jnp.broadcast_to(cs[:, 0:1], c.shape)
            c_scr[h] = c
            a = jnp.exp2(jnp.concatenate(parts[::-1], axis=1))
            if diag:
                a = jnp.where(causal, a, 0.0)
            acc_scr[h] += _dot(a.astype(BF16), v)

    @pl.when(qi == 0)
    def _():
        tile(0, True)

    @pl.when(qi >= 1)
    def _():
        tile(qi, True)
        tile(qi - 1, False)

    n_rest = jnp.maximum(qi - 1, 0)

    def body(jp, carry):
        tile(qi - 2 - 2 * jp, False)
        tile(qi - 3 - 2 * jp, False)
        return carry

    lax.fori_loop(0, n_rest // 2, body, 0)

    @pl.when(n_rest % 2 == 1)
    def _():
        tile(0, False)

    for h in range(hp):
        o_ref[0, :, h * hd:(h + 1) * hd] = (acc_scr[h] * _silu(z_ref[0, :, h * hd:(h + 1) * hd])).astype(o_ref.dtype)


def stick_breaking_branch(qkv, rest, t, hp):
    b, s, _ = qkv.shape
    t = _tile(s, t)
    hd = SB_HEAD_DIM
    w = hp * hd
    cpb = BRANCH_WIDTH // w
    return pl.pallas_call(
        functools.partial(_sb_kernel, t=t, hp=hp),
        grid=(b, SB_HEADS // hp, s // t),
        in_specs=[
            pl.BlockSpec((1, t, w), lambda bi, h, qi: (bi, qi, QKV_SB_Q * cpb + h)),
            pl.BlockSpec((1, s, w), lambda bi, h, qi: (bi, 0, QKV_SB_K * cpb + h)),
            pl.BlockSpec((1, s, w), lambda bi, h, qi: (bi, 0, QKV_SB_V * cpb + h)),
            pl.BlockSpec((1, t, w), lambda bi, h, qi: (bi, qi, REST_SB_Z * cpb + h)),
        ],
        out_specs=pl.BlockSpec((1, t, w), lambda bi, h, qi: (bi, qi, h)),
        out_shape=jax.ShapeDtypeStruct((b, s, BRANCH_WIDTH), BF16),
        scratch_shapes=[
            pltpu.VMEM((hp, s // t, hd, t), BF16),
            pltpu.VMEM((hp, t, hd), BF16),
            pltpu.VMEM((hp, t, LANES), F32),
            pltpu.VMEM((hp, t, hd), F32),
        ],
        compiler_params=_params("parallel", "parallel", "arbitrary"),
        name="stick_breaking",
    )(qkv, qkv, qkv, rest)


def _pool_kernel(u_ref, z_ref, w_ref, sc_ref, o_ref, ext_scr, *, tt):
    ti = pl.program_id(1)

    @pl.when(ti == 0)
    def _():
        ext_scr[0:POOL_HALO] = jnp.zeros((POOL_HALO, BRANCH_WIDTH), F32)

    ext_scr[POOL_HALO:POOL_HALO + tt] = u_ref[0]
    pos = ti * tt + lax.broadcasted_iota(jnp.int32, (tt, 1), 0)
    for g, w in enumerate(POOL_WINDOWS):
        cols = slice(g * POOL_GROUP, (g + 1) * POOL_GROUP)
        u = ext_scr[POOL_HALO:POOL_HALO + tt, cols]
        win = u
        for j in range(1, w):
            win = win + ext_scr[POOL_HALO - j:POOL_HALO - j + tt, cols]
        count = jnp.minimum(pos + 1, w).astype(F32)
        pooled = win / count - u
        mixed = _dot(pooled.astype(BF16), w_ref[g]) * sc_ref[:, cols]
        o_ref[0, :, cols] = (mixed * _silu(z_ref[0, :, cols])).astype(o_ref.dtype)
    ext_scr[0:POOL_HALO] = ext_scr[tt:tt + POOL_HALO]


def pool_branch(rest, w_pool_bf16, pool_scale, tt=512):
    b, s, _ = rest.shape
    tt = _tile(s, tt)
    assert tt >= POOL_HALO and max(POOL_WINDOWS) <= POOL_HALO
    return pl.pallas_call(
        functools.partial(_pool_kernel, tt=tt),
        grid=(b, s // tt),
        in_specs=[
            pl.BlockSpec((1, tt, BRANCH_WIDTH), lambda bi, ti: (bi, ti, REST_POOL_U)),
            pl.BlockSpec((1, tt, BRANCH_WIDTH), lambda bi, ti: (bi, ti, REST_POOL_Z)),
            pl.BlockSpec(w_pool_bf16.shape, lambda bi, ti: (0, 0, 0)),
            pl.BlockSpec((1, BRANCH_WIDTH), lambda bi, ti: (0, 0)),
        ],
        out_specs=pl.BlockSpec((1, tt, BRANCH_WIDTH), lambda bi, ti: (bi, ti, 0)),
        out_shape=jax.ShapeDtypeStruct((b, s, BRANCH_WIDTH), BF16),
        scratch_shapes=[pltpu.VMEM((tt + POOL_HALO, BRANCH_WIDTH), F32)],
        compiler_params=_params("parallel", "arbitrary"),
        name="multiscale_pool",
    )(rest, rest, w_pool_bf16, pool_scale.reshape(1, BRANCH_WIDTH))


def _mem_attn_kernel(q_ref, z_ref, mk_ref, mv_ref, o_ref):
    q = (q_ref[0] * (MEM_HEAD_DIM ** -0.5)).astype(BF16)
    s = _dot_nt(q, mk_ref[0].astype(BF16))
    p = jnp.exp(s - jnp.max(s, axis=-1, keepdims=True))
    l = jnp.sum(p, axis=-1, keepdims=True)
    o = _dot(p.astype(BF16), mv_ref[0].astype(BF16)) / l
    o_ref[0] = (o * _silu(z_ref[0])).astype(o_ref.dtype)


def memory_branch(qkv, rest, mkv, tq=1024):
    b, s, _ = qkv.shape
    m = mkv.shape[1]
    tq = _tile(s, tq)
    hd = MEM_HEAD_DIM
    cpb = BRANCH_WIDTH // hd
    return pl.pallas_call(
        _mem_attn_kernel,
        grid=(b, MEM_HEADS, s // tq),
        in_specs=[
            pl.BlockSpec((1, tq, hd), lambda bi, h, qi: (bi, qi, QKV_MEM_Q * cpb + h)),
            pl.BlockSpec((1, tq, hd), lambda bi, h, qi: (bi, qi, REST_MEM_Z * cpb + h)),
            pl.BlockSpec((1, m, hd), lambda bi, h, qi: (bi, 0, h)),
            pl.BlockSpec((1, m, hd), lambda bi, h, qi: (bi, 0, cpb + h)),
        ],
        out_specs=pl.BlockSpec((1, tq, hd), lambda bi, h, qi: (bi, qi, h)),
        out_shape=jax.ShapeDtypeStruct((b, s, BRANCH_WIDTH), BF16),
        compiler_params=_params("parallel", "parallel", "parallel"),
        name="memory_attention",
    )(qkv, rest, mkv, mkv)


def _merge_kernel(b0_ref, b1_ref, b2_ref, b3_ref, g0_ref, g1_ref, g2_ref, g3_ref, gb_ref, w_ref, o_ref):
    branches = (b0_ref, b1_ref, b2_ref, b3_ref)
    gates = (g0_ref, g1_ref, g2_ref, g3_ref)
    merged = None
    for n in range(N_BRANCH):
        gate = 1.0 / (1.0 + jnp.exp(-(gates[n][...] + gb_ref[n:n + 1, :])))
        term = gate * _dot(branches[n][...], w_ref[n])
        merged = term if merged is None else merged + term
    o_ref[...] = merged.astype(o_ref.dtype)


def gated_merge(branches, rest2d, gate_b, w_branch_bf16, tm=512, tn=1024):
    m = rest2d.shape[0]
    d = w_branch_bf16.shape[2]
    tm, tn = _tile(m, tm), _tile(d, tn)
    gate_col0 = REST_GATES * BRANCH_WIDTH // tn
    per_gate = d // tn
    br_spec = pl.BlockSpec((tm, BRANCH_WIDTH), lambda j, i: (i, 0))

    def gate_spec(n):
        return pl.BlockSpec((tm, tn), lambda j, i: (i, gate_col0 + n * per_gate + j))

    return pl.pallas_call(
        _merge_kernel,
        grid=(d // tn, m // tm),
        in_specs=[br_spec] * N_BRANCH + [gate_spec(n) for n in range(N_BRANCH)] + [
            pl.BlockSpec((N_BRANCH, tn), lambda j, i: (0, j)),
            pl.BlockSpec((N_BRANCH, BRANCH_WIDTH, tn), lambda j, i: (0, 0, j)),
        ],
        out_specs=pl.BlockSpec((tm, tn), lambda j, i: (i, j)),
        out_shape=jax.ShapeDtypeStruct((m, d), BF16),
        compiler_params=_params("parallel", "parallel"),
        name="gated_merge",
    )(*branches, rest2d, rest2d, rest2d, rest2d, gate_b, w_branch_bf16)


def _split_in_proj(w):
    cols = lambda sl: w[:, sl * BRANCH_WIDTH:(sl + 1) * BRANCH_WIDTH]
    w_qkv = jnp.concatenate([cols(sl) for sl in QKV_SLICES], axis=1)
    w_rest = jnp.concatenate([cols(sl) for sl in REST_SLICES] + [w[:, N_SLICES * BRANCH_WIDTH:]], axis=1)
    return w_qkv.astype(BF16), w_rest.astype(BF16)


def kernel(x, mem, rel_bias, norm_g, w_in, gate_b, lam_q1, lam_k1, lam_q2, lam_k2, da_norm_g, w_pool,
           pool_scale, mem_norm_g, w_mem_kv, w_branch, w_out, final_g):
    b, s, d = x.shape
    depth = norm_g.shape[0]
    n_mem = mem.shape[1]
    t = min(ATTN_TILE, s)
    hp = ATTN_HEADS_PER_STEP

    bias = bias_tiles(rel_bias, t)
    x2d = x.reshape(b * s, d)
    mem2d = mem.reshape(b * n_mem, d)
    for l in range(depth):
        lam_init = 0.8 - 0.6 * math.exp(-0.3 * l)
        w_qkv, w_rest = _split_in_proj(w_in[l])
        qkv = norm_matmul(x2d, norm_g[l], w_qkv, BF16).reshape(b, s, -1)
        rest2d = norm_matmul(x2d, norm_g[l], w_rest, F32)
        rest = rest2d.reshape(b, s, -1)

        lamv = jnp.stack([lam_q1[l], lam_k1[l], lam_q2[l], lam_k2[l]])
        br_da = diff_attention_branch(qkv, rest, bias, lamv, da_norm_g[l], lam_init, t, hp)
        br_sb = stick_breaking_branch(qkv, rest, t, hp)
        br_pool = pool_branch(rest, w_pool[l].astype(BF16), pool_scale[l])
        mkv = norm_matmul(mem2d, mem_norm_g[l], w_mem_kv[l].astype(BF16), F32)
        br_mem = memory_branch(qkv, rest, mkv.reshape(b, n_mem, 2 * BRANCH_WIDTH))

        branches = [br.reshape(b * s, BRANCH_WIDTH) for br in (br_da, br_sb, br_pool, br_mem)]
        merged = gated_merge(branches, rest2d, gate_b[l], w_branch[l].astype(BF16))
        x2d = matmul_residual(merged, w_out[l].astype(BF16), x2d, final_g if l == depth - 1 else None)
    return x2d.reshape(b, s, d)
```

```python
import functools
import math

import jax
import jax.numpy as jnp
from jax import lax
from jax.experimental import pallas as pl
from jax.experimental.pallas import tpu as pltpu

F32 = jnp.float32
BF16 = jnp.bfloat16

BRANCH_WIDTH = 1024
N_BRANCH = 4
N_SLICES = 12
DA_HEADS = 8
DA_QK_DIM = 64
DA_V_DIM = 2 * DA_QK_DIM
SB_HEADS = 8
SB_HEAD_DIM = BRANCH_WIDTH // SB_HEADS
POOL_WINDOWS = (2, 4, 8, 16)
POOL_GROUP = BRANCH_WIDTH // len(POOL_WINDOWS)
MEM_HEADS = 4
MEM_HEAD_DIM = BRANCH_WIDTH // MEM_HEADS
REL_BUCKETS = 32
REL_MAX_DIST = 128
EPS = 1e-6

SL_DA_Q, SL_DA_K, SL_DA_V, SL_DA_Z = 0, 1, 2, 3
SL_SB_Q, SL_SB_K, SL_SB_V, SL_SB_Z = 4, 5, 6, 7
SL_POOL_U, SL_POOL_Z, SL_MEM_Q, SL_MEM_Z = 8, 9, 10, 11
QKV_SLICES = (SL_DA_Q, SL_DA_K, SL_DA_V, SL_SB_Q, SL_SB_K, SL_SB_V, SL_MEM_Q)
REST_SLICES = (SL_DA_Z, SL_SB_Z, SL_POOL_U, SL_POOL_Z, SL_MEM_Z)
QKV_DA_Q, QKV_DA_K, QKV_DA_V, QKV_SB_Q, QKV_SB_K, QKV_SB_V, QKV_MEM_Q = range(len(QKV_SLICES))
REST_DA_Z, REST_SB_Z, REST_POOL_U, REST_POOL_Z, REST_MEM_Z, REST_GATES = range(len(REST_SLICES) + 1)

V7X_VMEM_LIMIT_BYTES = 56 * 1024 * 1024
LANES = 128
POOL_HALO = 16
NEG_BIG = -1e30
ATTN_TILE = 512
ATTN_HEADS_PER_STEP = 2
SB_CUMSUM_BLOCK = 256
LOG2E = 1.4426950408889634
LN2 = 0.6931471805599453


def _params(*sem):
    return pltpu.CompilerParams(dimension_semantics=sem, vmem_limit_bytes=V7X_VMEM_LIMIT_BYTES)


def _tile(n, t):
    t = min(t, n)
    assert n % t == 0, (n, t)
    return t


def _silu(z):
    return z * (1.0 / (1.0 + jnp.exp(-z)))


def _dot_nt(a, b):
    return lax.dot_general(a, b, (((1,), (1,)), ((), ())), preferred_element_type=F32)


def _dot(a, b):
    return jnp.dot(a, b, preferred_element_type=F32)


def _lane_tile(x, n):
    return x if n == 1 else jnp.concatenate([x] * n, axis=1)


def _neg_abs(x):
    u = lax.bitcast_convert_type(x, jnp.uint32) | jnp.uint32(0x80000000)
    return lax.bitcast_convert_type(u, F32)


def _norm_matmul_kernel(x_ref, g_ref, w_ref, o_ref, h_scr):
    @pl.when(pl.program_id(1) == 0)
    def _():
        x = x_ref[...]
        y = x * lax.rsqrt(jnp.mean(x * x, axis=-1, keepdims=True) + EPS)
        h_scr[...] = (y * g_ref[...]).astype(BF16)

    o_ref[...] = _dot(h_scr[...], w_ref[...]).astype(o_ref.dtype)


def _static_lookup(table, j):
    r = jnp.int32(table[0])
    for idx in range(1, len(table)):
        r = jnp.where(j == idx, jnp.int32(table[idx]), r)
    return r


def norm_matmul(x2d, gains, w_bf16, layer, col_blocks, out_dtype, tm=1024, tn=BRANCH_WIDTH):
    m, d = x2d.shape
    tm = _tile(m, tm)
    assert w_bf16.shape[2] % tn == 0
    n_out = len(col_blocks)
    return pl.pallas_call(
        _norm_matmul_kernel,
        grid=(m // tm, n_out),
        in_specs=[
            pl.BlockSpec((tm, d), lambda i, j: (i, 0)),
            pl.BlockSpec((None, 1, d), lambda i, j: (layer, 0, 0)),
            pl.BlockSpec((None, d, tn), lambda i, j: (layer, 0, _static_lookup(col_blocks, j))),
        ],
        out_specs=pl.BlockSpec((tm, tn), lambda i, j: (i, j)),
        out_shape=jax.ShapeDtypeStruct((m, n_out * tn), out_dtype),
        scratch_shapes=[pltpu.VMEM((tm, d), BF16)],
        compiler_params=_params("parallel", "arbitrary"),
        name="norm_matmul",
    )(x2d, gains.reshape(-1, 1, d), w_bf16)


def _matmul_residual_kernel(a_ref, w_ref, r_ref, g_ref, o_ref, *, final_norm):
    y = r_ref[...] + _dot(a_ref[...], w_ref[...])
    if final_norm:
        y = (y * lax.rsqrt(jnp.mean(y * y, axis=-1, keepdims=True) + EPS)) * g_ref[...]
    o_ref[...] = y


def matmul_residual(a_bf16, w_bf16, layer, res, norm_g, final_norm, tm=512):
    m, k = a_bf16.shape
    n = w_bf16.shape[2]
    tm = _tile(m, tm)
    g = norm_g.reshape(1, n)
    return pl.pallas_call(
        functools.partial(_matmul_residual_kernel, final_norm=final_norm),
        grid=(m // tm,),
        in_specs=[
            pl.BlockSpec((tm, k), lambda i: (i, 0)),
            pl.BlockSpec((None, k, n), lambda i: (layer, 0, 0)),
            pl.BlockSpec((tm, n), lambda i: (i, 0)),
            pl.BlockSpec((1, n), lambda i: (0, 0)),
        ],
        out_specs=pl.BlockSpec((tm, n), lambda i: (i, 0)),
        out_shape=jax.ShapeDtypeStruct((m, n), F32),
        compiler_params=_params("parallel"),
        name="out_proj_residual",
    )(a_bf16, w_bf16, res, g)


def _bias_tiles_kernel(rb_ref, o_ref, *, t):
    h = pl.program_id(0)
    qi = lax.broadcasted_iota(jnp.int32, (t, t), 0)
    ki = lax.broadcasted_iota(jnp.int32, (t, t), 1)
    max_exact = REL_BUCKETS // 2
    far = rb_ref[REL_BUCKETS - 1, h]
    for sel in range(2):
        rel = qi - ki + sel * t
        n = jnp.maximum(rel, 0)
        nf = jnp.maximum(n, 1).astype(F32)
        large = max_exact + (jnp.log(nf / max_exact) / math.log(REL_MAX_DIST / max_exact)
                             * (REL_BUCKETS - max_exact)).astype(jnp.int32)
        large = jnp.minimum(large, REL_BUCKETS - 1)
        bucket = jnp.where(n < max_exact, n, large)
        val = jnp.zeros((t, t), F32)
        for b in range(REL_BUCKETS - 1):
            val = jnp.where(bucket == b, rb_ref[b, h] - far, val)
        if sel == 0:
            val = jnp.where(rel >= 0, val, NEG_BIG)
        o_ref[0, sel] = val


def bias_tiles(rel_bias, t):
    assert t >= REL_MAX_DIST
    heads = rel_bias.shape[1]
    return pl.pallas_call(
        functools.partial(_bias_tiles_kernel, t=t),
        grid=(heads,),
        in_specs=[pl.BlockSpec(memory_space=pltpu.SMEM)],
        out_specs=pl.BlockSpec((1, 2, t, t), lambda h: (h, 0, 0, 0)),
        out_shape=jax.ShapeDtypeStruct((heads, 2, t, t), F32),
        compiler_params=_params("parallel"),
        name="t5_bias_tiles",
    )(rel_bias)


def _da_kernel(lamv_ref, q_ref, k_ref, v_ref, z_ref, bias_ref, g_ref, o_ref,
               kt_scr, v_scr, q_scr, m_scr, acc_scr, *, t, hp, lam_init):
    qi = pl.program_id(2)
    nt = kt_scr.shape[1]
    hd = DA_V_DIM

    @pl.when(qi == 0)
    def _():
        for h in range(hp):
            for jj in range(nt):
                kt_scr[h, jj] = k_ref[0, jj * t:(jj + 1) * t, h * hd:(h + 1) * hd].astype(F32).T.astype(BF16)
            v_scr[h, :, 0:hd] = v_ref[0, :, h * hd:(h + 1) * hd]
            v_scr[h, :, hd:2 * hd] = jnp.ones((v_scr.shape[1], hd), BF16)

    lane = lax.broadcasted_iota(jnp.int32, (t, hd), 1)
    for h in range(hp):
        q = q_ref[0, :, h * hd:(h + 1) * hd] * (DA_QK_DIM ** -0.5)
        q_scr[h, 0] = jnp.where(lane < DA_QK_DIM, q, 0.0).astype(BF16)
        q_scr[h, 1] = jnp.where(lane >= DA_QK_DIM, q, 0.0).astype(BF16)
    m_scr[...] = jnp.full(m_scr.shape, -jnp.inf, F32)
    acc_scr[...] = jnp.zeros(acc_scr.shape, F32)

    def tile(j, bias_sel):
        off = pl.multiple_of(j * t, t)
        for h in range(hp):
            kt = kt_scr[h, j]
            v = v_scr[h, pl.ds(off, t), :]
            for c in range(2):
                s = _dot(q_scr[h, c], kt)
                if bias_sel is not None:
                    s = s + bias_ref[h, bias_sel]
                m_prev = m_scr[h, c]
                m_new = jnp.maximum(m_prev, jnp.max(s, axis=-1, keepdims=True))
                alpha = jnp.exp(m_prev - m_new)
                p = jnp.exp(s - _lane_tile(m_new, t // LANES))
                acc_scr[h, c] = _lane_tile(alpha, 2) * acc_scr[h, c] + _dot(p.astype(BF16), v)
                m_scr[h, c] = m_new

    n_far = jnp.maximum(qi - 1, 0)

    def far_body(jp, carry):
        tile(2 * jp, None)
        tile(2 * jp + 1, None)
        return carry

    lax.fori_loop(0, n_far // 2, far_body, 0)

    @pl.when(n_far % 2 == 1)
    def _():
        tile(n_far - 1, None)

    @pl.when(qi >= 1)
    def _():
        tile(qi - 1, 1)
        tile(qi, 0)

    @pl.when(qi == 0)
    def _():
        tile(0, 0)

    lv = lamv_ref[...]
    s1 = jnp.sum(lv[0:1] * lv[1:2], axis=-1, keepdims=True)
    s2 = jnp.sum(lv[2:3] * lv[3:4], axis=-1, keepdims=True)
    lam = jnp.exp(s1) - jnp.exp(s2) + lam_init
    for h in range(hp):
        a0 = acc_scr[h, 0]
        a1 = acc_scr[h, 1]
        o = a0[:, 0:hd] / a0[:, hd:2 * hd] - lam * (a1[:, 0:hd] / a1[:, hd:2 * hd])
        y = o * lax.rsqrt(jnp.mean(o * o, axis=-1, keepdims=True) + EPS)
        y = (y * g_ref[:, h * hd:(h + 1) * hd]) * (1.0 - lam_init)
        o_ref[0, :, h * hd:(h + 1) * hd] = (y * _silu(z_ref[0, :, h * hd:(h + 1) * hd])).astype(o_ref.dtype)


def diff_attention_branch(qkv, rest, bias, lamv, da_norm_g, layer, lam_init, t, hp):
    b, s, _ = qkv.shape
    t = _tile(s, t)
    hd = DA_V_DIM
    w = hp * hd
    cpb = BRANCH_WIDTH // w
    return pl.pallas_call(
        functools.partial(_da_kernel, t=t, hp=hp, lam_init=lam_init),
        grid=(b, DA_HEADS // hp, s // t),
        in_specs=[
            pl.BlockSpec((None, 4, DA_QK_DIM), lambda bi, h, qi: (layer, 0, 0)),
            pl.BlockSpec((1, t, w), lambda bi, h, qi: (bi, qi, QKV_DA_Q * cpb + h)),
            pl.BlockSpec((1, s, w), lambda bi, h, qi: (bi, 0, QKV_DA_K * cpb + h)),
            pl.BlockSpec((1, s, w), lambda bi, h, qi: (bi, 0, QKV_DA_V * cpb + h)),
            pl.BlockSpec((1, t, w), lambda bi, h, qi: (bi, qi, REST_DA_Z * cpb + h)),
            pl.BlockSpec((hp, 2, t, t), lambda bi, h, qi: (h, 0, 0, 0)),
            pl.BlockSpec((None, 1, w), lambda bi, h, qi: (layer, 0, h)),
        ],
        out_specs=pl.BlockSpec((1, t, w), lambda bi, h, qi: (bi, qi, h)),
        out_shape=jax.ShapeDtypeStruct((b, s, BRANCH_WIDTH), BF16),
        scratch_shapes=[
            pltpu.VMEM((hp, s // t, hd, t), BF16),
            pltpu.VMEM((hp, s, 2 * hd), BF16),
            pltpu.VMEM((hp, 2, t, hd), BF16),
            pltpu.VMEM((hp, 2, t, LANES), F32),
            pltpu.VMEM((hp, 2, t, 2 * hd), F32),
        ],
        compiler_params=_params("parallel", "parallel", "arbitrary"),
        name="diff_attention",
    )(lamv, qkv, qkv, qkv, rest, bias, da_norm_g.reshape(-1, 1, BRANCH_WIDTH))


def _sb_kernel(q_ref, k_ref, v_ref, z_ref, o_ref, kt_scr, q_scr, c_scr, acc_scr, *, t, hp):
    qi = pl.program_id(2)
    nt = kt_scr.shape[1]
    hd = SB_HEAD_DIM
    cb = min(SB_CUMSUM_BLOCK, t)

    @pl.when(qi == 0)
    def _():
        for h in range(hp):
            for jj in range(nt):
                kt_scr[h, jj] = k_ref[0, jj * t:(jj + 1) * t, h * hd:(h + 1) * hd].astype(F32).T.astype(BF16)

    for h in range(hp):
        q_scr[h] = (q_ref[0, :, h * hd:(h + 1) * hd].astype(F32) * (SB_HEAD_DIM ** -0.5 * LOG2E)).astype(BF16)
    c_scr[...] = jnp.zeros(c_scr.shape, F32)
    acc_scr[...] = jnp.zeros(acc_scr.shape, F32)

    def tile(j, diag):
        off = pl.multiple_of(j * t, t)
        r2 = lax.broadcasted_iota(jnp.int32, (cb, cb), 0)
        c2 = lax.broadcasted_iota(jnp.int32, (cb, cb), 1)
        upper = jnp.where(r2 >= c2, 1.0, 0.0).astype(BF16)
        if diag:
            row = lax.broadcasted_iota(jnp.int32, (t, t), 0)
            col = lax.broadcasted_iota(jnp.int32, (t, t), 1)
            causal = col < row
        for h in range(hp):
            v = v_ref[0, pl.ds(off, t), h * hd:(h + 1) * hd]
            z2 = _dot(q_scr[h], kt_scr[h, j])
            w = jnp.maximum(z2, 0.0) + jnp.log(1.0 + jnp.exp2(_neg_abs(z2))) * (1.0 / LN2)
            if diag:
                w = jnp.where(causal, w, 0.0)
            c = c_scr[h]
            parts = []
            for blk in reversed(range(t // cb)):
                sl = slice(blk * cb, (blk + 1) * cb)
                cs = _dot(w[:, sl].astype(BF16), upper)
                parts.append((z2[:, sl] - _lane_tile(c, cb // LANES)) - cs)
                c = c + jnp.broadcast_to(cs[:, 0:1], c.shape)
            c_scr[h] = c
            a = jnp.exp2(jnp.concatenate(parts[::-1], axis=1))
            if diag:
                a = jnp.where(causal, a, 0.0)
            acc_scr[h] += _dot(a.astype(BF16), v)

    @pl.when(qi == 0)
    def _():
        tile(0, True)

    @pl.when(qi >= 1)
    def _():
        tile(qi, True)
        tile(qi - 1, False)

    n_rest = jnp.maximum(qi - 1, 0)

    def body(jp, carry):
        tile(qi - 2 - 2 * jp, False)
        tile(qi - 3 - 2 * jp, False)
        return carry

    lax.fori_loop(0, n_rest // 2, body, 0)

    @pl.when(n_rest % 2 == 1)
    def _():
        tile(0, False)

    for h in range(hp):
        o_ref[0, :, h * hd:(h + 1) * hd] = (acc_scr[h] * _silu(z_ref[0, :, h * hd:(h + 1) * hd])).astype(o_ref.dtype)


def stick_breaking_branch(qkv, rest, t, hp):
    b, s, _ = qkv.shape
    t = _tile(s, t)
    hd = SB_HEAD_DIM
    w = hp * hd
    cpb = BRANCH_WIDTH // w
    return pl.pallas_call(
        functools.partial(_sb_kernel, t=t, hp=hp),
        grid=(b, SB_HEADS // hp, s // t),
        in_specs=[
            pl.BlockSpec((1, t, w), lambda bi, h, qi: (bi, qi, QKV_SB_Q * cpb + h)),
            pl.BlockSpec((1, s, w), lambda bi, h, qi: (bi, 0, QKV_SB_K * cpb + h)),
            pl.BlockSpec((1, s, w), lambda bi, h, qi: (bi, 0, QKV_SB_V * cpb + h)),
            pl.BlockSpec((1, t, w), lambda bi, h, qi: (bi, qi, REST_SB_Z * cpb + h)),
        ],
        out_specs=pl.BlockSpec((1, t, w), lambda bi, h, qi: (bi, qi, h)),
        out_shape=jax.ShapeDtypeStruct((b, s, BRANCH_WIDTH), BF16),
        scratch_shapes=[
            pltpu.VMEM((hp, s // t, hd, t), BF16),
            pltpu.VMEM((hp, t, hd), BF16),
            pltpu.VMEM((hp, t, LANES), F32),
            pltpu.VMEM((hp, t, hd), F32),
        ],
        compiler_params=_params("parallel", "parallel", "arbitrary"),
        name="stick_breaking",
    )(qkv, qkv, qkv, rest)


def _pool_kernel(u_ref, z_ref, w_ref, sc_ref, o_ref, ext_scr, *, tt):
    ti = pl.program_id(1)

    @pl.when(ti == 0)
    def _():
        ext_scr[0:POOL_HALO] = jnp.zeros((POOL_HALO, BRANCH_WIDTH), F32)

    ext_scr[POOL_HALO:POOL_HALO + tt] = u_ref[0]
    pos = ti * tt + lax.broadcasted_iota(jnp.int32, (tt, 1), 0)
    for g, w in enumerate(POOL_WINDOWS):
        cols = slice(g * POOL_GROUP, (g + 1) * POOL_GROUP)
        u = ext_scr[POOL_HALO:POOL_HALO + tt, cols]
        win = u
        for j in range(1, w):
            win = win + ext_scr[POOL_HALO - j:POOL_HALO - j + tt, cols]
        count = jnp.minimum(pos + 1, w).astype(F32)
        pooled = win / count - u
        mixed = _dot(pooled.astype(BF16), w_ref[g]) * sc_ref[:, cols]
        o_ref[0, :, cols] = (mixed * _silu(z_ref[0, :, cols])).astype(o_ref.dtype)
    ext_scr[0:POOL_HALO] = ext_scr[tt:tt + POOL_HALO]


def pool_branch(rest, w_pool_bf16, pool_scale, layer, tt=512):
    b, s, _ = rest.shape
    tt = _tile(s, tt)
    assert tt >= POOL_HALO and max(POOL_WINDOWS) <= POOL_HALO
    return pl.pallas_call(
        functools.partial(_pool_kernel, tt=tt),
        grid=(b, s // tt),
        in_specs=[
            pl.BlockSpec((1, tt, BRANCH_WIDTH), lambda bi, ti: (bi, ti, REST_POOL_U)),
            pl.BlockSpec((1, tt, BRANCH_WIDTH), lambda bi, ti: (bi, ti, REST_POOL_Z)),
            pl.BlockSpec((None,) + w_pool_bf16.shape[1:], lambda bi, ti: (layer, 0, 0, 0)),
            pl.BlockSpec((None, 1, BRANCH_WIDTH), lambda bi, ti: (layer, 0, 0)),
        ],
        out_specs=pl.BlockSpec((1, tt, BRANCH_WIDTH), lambda bi, ti: (bi, ti, 0)),
        out_shape=jax.ShapeDtypeStruct((b, s, BRANCH_WIDTH), BF16),
        scratch_shapes=[pltpu.VMEM((tt + POOL_HALO, BRANCH_WIDTH), F32)],
        compiler_params=_params("parallel", "arbitrary"),
        name="multiscale_pool",
    )(rest, rest, w_pool_bf16, pool_scale.reshape(-1, 1, BRANCH_WIDTH))


def _mem_attn_kernel(q_ref, z_ref, mk_ref, mv_ref, o_ref):
    q = (q_ref[0] * (MEM_HEAD_DIM ** -0.5)).astype(BF16)
    s = _dot_nt(q, mk_ref[0].astype(BF16))
    p = jnp.exp(s - jnp.max(s, axis=-1, keepdims=True))
    l = jnp.sum(p, axis=-1, keepdims=True)
    o = _dot(p.astype(BF16), mv_ref[0].astype(BF16)) / l
    o_ref[0] = (o * _silu(z_ref[0])).astype(o_ref.dtype)


def memory_branch(qkv, rest, mkv, tq=1024):
    b, s, _ = qkv.shape
    m = mkv.shape[1]
    tq = _tile(s, tq)
    hd = MEM_HEAD_DIM
    cpb = BRANCH_WIDTH // hd
    return pl.pallas_call(
        _mem_attn_kernel,
        grid=(b, MEM_HEADS, s // tq),
        in_specs=[
            pl.BlockSpec((1, tq, hd), lambda bi, h, qi: (bi, qi, QKV_MEM_Q * cpb + h)),
            pl.BlockSpec((1, tq, hd), lambda bi, h, qi: (bi, qi, REST_MEM_Z * cpb + h)),
            pl.BlockSpec((1, m, hd), lambda bi, h, qi: (bi, 0, h)),
            pl.BlockSpec((1, m, hd), lambda bi, h, qi: (bi, 0, cpb + h)),
        ],
        out_specs=pl.BlockSpec((1, tq, hd), lambda bi, h, qi: (bi, qi, h)),
        out_shape=jax.ShapeDtypeStruct((b, s, BRANCH_WIDTH), BF16),
        compiler_params=_params("parallel", "parallel", "parallel"),
        name="memory_attention",
    )(qkv, rest, mkv, mkv)


def _merge_kernel(b0_ref, b1_ref, b2_ref, b3_ref, g0_ref, g1_ref, g2_ref, g3_ref, gb_ref, w_ref, o_ref):
    branches = (b0_ref, b1_ref, b2_ref, b3_ref)
    gates = (g0_ref, g1_ref, g2_ref, g3_ref)
    merged = None
    for n in range(N_BRANCH):
        gate = 1.0 / (1.0 + jnp.exp(-(gates[n][...] + gb_ref[n:n + 1, :])))
        term = gate * _dot(branches[n][...], w_ref[n])
        merged = term if merged is None else merged + term
    o_ref[...] = merged.astype(o_ref.dtype)


def gated_merge(branches, rest2d, gate_b, w_branch_bf16, layer, tm=512, tn=1024):
    m = rest2d.shape[0]
    d = w_branch_bf16.shape[3]
    tm, tn = _tile(m, tm), _tile(d, tn)
    gate_col0 = REST_GATES * BRANCH_WIDTH // tn
    per_gate = d // tn
    br_spec = pl.BlockSpec((tm, BRANCH_WIDTH), lambda j, i: (i, 0))

    def gate_spec(n):
        return pl.BlockSpec((tm, tn), lambda j, i: (i, gate_col0 + n * per_gate + j))

    return pl.pallas_call(
        _merge_kernel,
        grid=(d // tn, m // tm),
        in_specs=[br_spec] * N_BRANCH + [gate_spec(n) for n in range(N_BRANCH)] + [
            pl.BlockSpec((None, N_BRANCH, tn), lambda j, i: (layer, 0, j)),
            pl.BlockSpec((None, N_BRANCH, BRANCH_WIDTH, tn), lambda j, i: (layer, 0, 0, j)),
        ],
        out_specs=pl.BlockSpec((tm, tn), lambda j, i: (i, j)),
        out_shape=jax.ShapeDtypeStruct((m, d), BF16),
        compiler_params=_params("parallel", "parallel"),
        name="gated_merge",
    )(*branches, rest2d, rest2d, rest2d, rest2d, gate_b, w_branch_bf16)


def kernel(x, mem, rel_bias, norm_g, w_in, gate_b, lam_q1, lam_k1, lam_q2, lam_k2, da_norm_g, w_pool,
           pool_scale, mem_norm_g, w_mem_kv, w_branch, w_out, final_g):
    b, s, d = x.shape
    depth = norm_g.shape[0]
    n_mem = mem.shape[1]
    t = min(ATTN_TILE, s)
    hp = ATTN_HEADS_PER_STEP

    w_in_bf, w_pool_bf, w_mem_kv_bf = w_in.astype(BF16), w_pool.astype(BF16), w_mem_kv.astype(BF16)
    w_branch_bf, w_out_bf = w_branch.astype(BF16), w_out.astype(BF16)
    lamv = jnp.stack([lam_q1, lam_k1, lam_q2, lam_k2], axis=1)
    n_gate_blocks = (w_in.shape[2] - N_SLICES * BRANCH_WIDTH) // BRANCH_WIDTH
    rest_blocks = REST_SLICES + tuple(range(N_SLICES, N_SLICES + n_gate_blocks))
    mkv_blocks = tuple(range(w_mem_kv.shape[2] // BRANCH_WIDTH))

    bias = bias_tiles(rel_bias, t)
    x2d = x.reshape(b * s, d)
    mem2d = mem.reshape(b * n_mem, d)
    for l in range(depth):
        lam_init = 0.8 - 0.6 * math.exp(-0.3 * l)
        qkv = norm_matmul(x2d, norm_g, w_in_bf, l, QKV_SLICES, BF16).reshape(b, s, -1)
        rest2d = norm_matmul(x2d, norm_g, w_in_bf, l, rest_blocks, F32)
        rest = rest2d.reshape(b, s, -1)

        br_da = diff_attention_branch(qkv, rest, bias, lamv, da_norm_g, l, lam_init, t, hp)
        br_sb = stick_breaking_branch(qkv, rest, t, hp)
        br_pool = pool_branch(rest, w_pool_bf, pool_scale, l)
        mkv = norm_matmul(mem2d, mem_norm_g, w_mem_kv_bf, l, mkv_blocks, F32)
        br_mem = memory_branch(qkv, rest, mkv.reshape(b, n_mem, 2 * BRANCH_WIDTH))

        branches = [br.reshape(b * s, BRANCH_WIDTH) for br in (br_da, br_sb, br_pool, br_mem)]
        merged = gated_merge(branches, rest2d, gate_b, w_branch_bf, l)
        x2d = matmul_residual(merged, w_out_bf, l, x2d, final_g, final_norm=(l == depth - 1))
    return x2d.reshape(b, s, d)
```

```python
import functools
import math

import jax
import jax.numpy as jnp
from jax import lax
from jax.experimental import pallas as pl
from jax.experimental.pallas import tpu as pltpu

F32 = jnp.float32
BF16 = jnp.bfloat16

BRANCH_WIDTH = 1024
N_BRANCH = 4
N_SLICES = 12
DA_HEADS = 8
DA_QK_DIM = 64
DA_V_DIM = 2 * DA_QK_DIM
SB_HEADS = 8
SB_HEAD_DIM = BRANCH_WIDTH // SB_HEADS
POOL_WINDOWS = (2, 4, 8, 16)
POOL_GROUP = BRANCH_WIDTH // len(POOL_WINDOWS)
MEM_HEADS = 4
MEM_HEAD_DIM = BRANCH_WIDTH // MEM_HEADS
REL_BUCKETS = 32
REL_MAX_DIST = 128
EPS = 1e-6

SL_DA_Q, SL_DA_K, SL_DA_V, SL_DA_Z = 0, 1, 2, 3
SL_SB_Q, SL_SB_K, SL_SB_V, SL_SB_Z = 4, 5, 6, 7
SL_POOL_U, SL_POOL_Z, SL_MEM_Q, SL_MEM_Z = 8, 9, 10, 11
QKV_SLICES = (SL_DA_Q, SL_DA_K, SL_DA_V, SL_SB_Q, SL_SB_K, SL_SB_V, SL_MEM_Q)
REST_SLICES = (SL_DA_Z, SL_SB_Z, SL_POOL_U, SL_POOL_Z, SL_MEM_Z)
QKV_DA_Q, QKV_DA_K, QKV_DA_V, QKV_SB_Q, QKV_SB_K, QKV_SB_V, QKV_MEM_Q = range(len(QKV_SLICES))
REST_DA_Z, REST_SB_Z, REST_POOL_U, REST_POOL_Z, REST_MEM_Z, REST_GATES = range(len(REST_SLICES) + 1)

V7X_VMEM_LIMIT_BYTES = 56 * 1024 * 1024
LANES = 128
POOL_HALO = 16
NEG_BIG = -1e30
ATTN_TILE = 512
ATTN_HEADS_PER_STEP = 2
SB_CUMSUM_BLOCK = 256
LOG2E = 1.4426950408889634
LN2 = 0.6931471805599453


def _params(*sem):
    return pltpu.CompilerParams(dimension_semantics=sem, vmem_limit_bytes=V7X_VMEM_LIMIT_BYTES)


def _tile(n, t):
    t = min(t, n)
    assert n % t == 0, (n, t)
    return t


def _silu(z):
    return z * (1.0 / (1.0 + jnp.exp(-z)))


def _dot_nt(a, b):
    return lax.dot_general(a, b, (((1,), (1,)), ((), ())), preferred_element_type=F32)


def _dot(a, b):
    return jnp.dot(a, b, preferred_element_type=F32)


def _lane_tile(x, n):
    return x if n == 1 else jnp.concatenate([x] * n, axis=1)


def _neg_abs(x):
    u = lax.bitcast_convert_type(x, jnp.uint32) | jnp.uint32(0x80000000)
    return lax.bitcast_convert_type(u, F32)


def _norm_matmul_kernel(x_ref, g_ref, w_ref, o_ref, h_scr):
    @pl.when(pl.program_id(1) == 0)
    def _():
        x = x_ref[...]
        y = x * lax.rsqrt(jnp.mean(x * x, axis=-1, keepdims=True) + EPS)
        h_scr[...] = (y * g_ref[...]).astype(BF16)

    o_ref[...] = _dot(h_scr[...], w_ref[...].astype(BF16)).astype(o_ref.dtype)


def _static_lookup(table, j):
    r = jnp.int32(table[0])
    for idx in range(1, len(table)):
        r = jnp.where(j == idx, jnp.int32(table[idx]), r)
    return r


def norm_matmul(x2d, gains, w, layer, col_blocks, out_dtype, tm=1024, tn=BRANCH_WIDTH):
    m, d = x2d.shape
    tm = _tile(m, tm)
    assert w.shape[2] % tn == 0
    n_out = len(col_blocks)
    return pl.pallas_call(
        _norm_matmul_kernel,
        grid=(m // tm, n_out),
        in_specs=[
            pl.BlockSpec((tm, d), lambda i, j: (i, 0)),
            pl.BlockSpec((None, 1, d), lambda i, j: (layer, 0, 0)),
            pl.BlockSpec((None, d, tn), lambda i, j: (layer, 0, _static_lookup(col_blocks, j))),
        ],
        out_specs=pl.BlockSpec((tm, tn), lambda i, j: (i, j)),
        out_shape=jax.ShapeDtypeStruct((m, n_out * tn), out_dtype),
        scratch_shapes=[pltpu.VMEM((tm, d), BF16)],
        compiler_params=_params("parallel", "arbitrary"),
        name="norm_matmul",
    )(x2d, gains.reshape(-1, 1, d), w)


def _matmul_residual_kernel(a_ref, w_ref, r_ref, g_ref, o_ref, *, final_norm):
    y = r_ref[...] + _dot(a_ref[...], w_ref[...])
    if final_norm:
        y = (y * lax.rsqrt(jnp.mean(y * y, axis=-1, keepdims=True) + EPS)) * g_ref[...]
    o_ref[...] = y


def matmul_residual(a_bf16, w_bf16, layer, res, norm_g, final_norm, tm=512):
    m, k = a_bf16.shape
    n = w_bf16.shape[2]
    tm = _tile(m, tm)
    g = norm_g.reshape(1, n)
    return pl.pallas_call(
        functools.partial(_matmul_residual_kernel, final_norm=final_norm),
        grid=(m // tm,),
        in_specs=[
            pl.BlockSpec((tm, k), lambda i: (i, 0)),
            pl.BlockSpec((None, k, n), lambda i: (layer, 0, 0)),
            pl.BlockSpec((tm, n), lambda i: (i, 0)),
            pl.BlockSpec((1, n), lambda i: (0, 0)),
        ],
        out_specs=pl.BlockSpec((tm, n), lambda i: (i, 0)),
        out_shape=jax.ShapeDtypeStruct((m, n), F32),
        compiler_params=_params("parallel"),
        name="out_proj_residual",
    )(a_bf16, w_bf16, res, g)


def _bias_tiles_kernel(rb_ref, o_ref, *, t):
    h = pl.program_id(0)
    qi = lax.broadcasted_iota(jnp.int32, (t, t), 0)
    ki = lax.broadcasted_iota(jnp.int32, (t, t), 1)
    max_exact = REL_BUCKETS // 2
    far = rb_ref[REL_BUCKETS - 1, h]
    for sel in range(2):
        rel = qi - ki + sel * t
        n = jnp.maximum(rel, 0)
        nf = jnp.maximum(n, 1).astype(F32)
        large = max_exact + (jnp.log(nf / max_exact) / math.log(REL_MAX_DIST / max_exact)
                             * (REL_BUCKETS - max_exact)).astype(jnp.int32)
        large = jnp.minimum(large, REL_BUCKETS - 1)
        bucket = jnp.where(n < max_exact, n, large)
        val = jnp.zeros((t, t), F32)
        for b in range(REL_BUCKETS - 1):
            val = jnp.where(bucket == b, rb_ref[b, h] - far, val)
        if sel == 0:
            val = jnp.where(rel >= 0, val, NEG_BIG)
        o_ref[0, sel] = val


def bias_tiles(rel_bias, t):
    assert t >= REL_MAX_DIST
    heads = rel_bias.shape[1]
    return pl.pallas_call(
        functools.partial(_bias_tiles_kernel, t=t),
        grid=(heads,),
        in_specs=[pl.BlockSpec(memory_space=pltpu.SMEM)],
        out_specs=pl.BlockSpec((1, 2, t, t), lambda h: (h, 0, 0, 0)),
        out_shape=jax.ShapeDtypeStruct((heads, 2, t, t), F32),
        compiler_params=_params("parallel"),
        name="t5_bias_tiles",
    )(rel_bias)


def _da_kernel(lamv_ref, q_ref, k_ref, v_ref, z_ref, bias_ref, g_ref, o_ref,
               kt_scr, v_scr, q_scr, m_scr, acc_scr, *, t, hp, lam_init):
    qi = pl.program_id(2)
    nt = kt_scr.shape[1]
    hd = DA_V_DIM

    @pl.when(qi == 0)
    def _():
        for h in range(hp):
            for jj in range(nt):
                kt_scr[h, jj] = k_ref[0, jj * t:(jj + 1) * t, h * hd:(h + 1) * hd].astype(F32).T.astype(BF16)
            v_scr[h, :, 0:hd] = v_ref[0, :, h * hd:(h + 1) * hd]
            v_scr[h, :, hd:2 * hd] = jnp.ones((v_scr.shape[1], hd), BF16)

    lane = lax.broadcasted_iota(jnp.int32, (t, hd), 1)
    for h in range(hp):
        q = q_ref[0, :, h * hd:(h + 1) * hd] * (DA_QK_DIM ** -0.5)
        q_scr[h, 0] = jnp.where(lane < DA_QK_DIM, q, 0.0).astype(BF16)
        q_scr[h, 1] = jnp.where(lane >= DA_QK_DIM, q, 0.0).astype(BF16)
    m_scr[...] = jnp.full(m_scr.shape, -jnp.inf, F32)
    acc_scr[...] = jnp.zeros(acc_scr.shape, F32)

    def tile(j, bias_sel):
        off = pl.multiple_of(j * t, t)
        for h in range(hp):
            kt = kt_scr[h, j]
            v = v_scr[h, pl.ds(off, t), :]
            for c in range(2):
                s = _dot(q_scr[h, c], kt)
                if bias_sel is not None:
                    s = s + bias_ref[h, bias_sel]
                m_prev = m_scr[h, c]
                m_new = jnp.maximum(m_prev, jnp.max(s, axis=-1, keepdims=True))
                alpha = jnp.exp(m_prev - m_new)
                p = jnp.exp(s - _lane_tile(m_new, t // LANES))
                acc_scr[h, c] = _lane_tile(alpha, 2) * acc_scr[h, c] + _dot(p.astype(BF16), v)
                m_scr[h, c] = m_new

    n_far = jnp.maximum(qi - 1, 0)

    def far_body(jp, carry):
        tile(2 * jp, None)
        tile(2 * jp + 1, None)
        return carry

    lax.fori_loop(0, n_far // 2, far_body, 0)

    @pl.when(n_far % 2 == 1)
    def _():
        tile(n_far - 1, None)

    @pl.when(qi >= 1)
    def _():
        tile(qi - 1, 1)
        tile(qi, 0)

    @pl.when(qi == 0)
    def _():
        tile(0, 0)

    lv = lamv_ref[...]
    s1 = jnp.sum(lv[0:1] * lv[1:2], axis=-1, keepdims=True)
    s2 = jnp.sum(lv[2:3] * lv[3:4], axis=-1, keepdims=True)
    lam = jnp.exp(s1) - jnp.exp(s2) + lam_init
    for h in range(hp):
        a0 = acc_scr[h, 0]
        a1 = acc_scr[h, 1]
        o = a0[:, 0:hd] / a0[:, hd:2 * hd] - lam * (a1[:, 0:hd] / a1[:, hd:2 * hd])
        y = o * lax.rsqrt(jnp.mean(o * o, axis=-1, keepdims=True) + EPS)
        y = (y * g_ref[:, h * hd:(h + 1) * hd]) * (1.0 - lam_init)
        o_ref[0, :, h * hd:(h + 1) * hd] = (y * _silu(z_ref[0, :, h * hd:(h + 1) * hd])).astype(o_ref.dtype)


def diff_attention_branch(qkv, rest, bias, lamv, da_norm_g, layer, lam_init, t, hp):
    b, s, _ = qkv.shape
    t = _tile(s, t)
    hd = DA_V_DIM
    w = hp * hd
    cpb = BRANCH_WIDTH // w
    return pl.pallas_call(
        functools.partial(_da_kernel, t=t, hp=hp, lam_init=lam_init),
        grid=(b, DA_HEADS // hp, s // t),
        in_specs=[
            pl.BlockSpec((None, 4, DA_QK_DIM), lambda bi, h, qi: (layer, 0, 0)),
            pl.BlockSpec((1, t, w), lambda bi, h, qi: (bi, qi, QKV_DA_Q * cpb + h)),
            pl.BlockSpec((1, s, w), lambda bi, h, qi: (bi, 0, QKV_DA_K * cpb + h)),
            pl.BlockSpec((1, s, w), lambda bi, h, qi: (bi, 0, QKV_DA_V * cpb + h)),
            pl.BlockSpec((1, t, w), lambda bi, h, qi: (bi, qi, REST_DA_Z * cpb + h)),
            pl.BlockSpec((hp, 2, t, t), lambda bi, h, qi: (h, 0, 0, 0)),
            pl.BlockSpec((None, 1, w), lambda bi, h, qi: (layer, 0, h)),
        ],
        out_specs=pl.BlockSpec((1, t, w), lambda bi, h, qi: (bi, qi, h)),
        out_shape=jax.ShapeDtypeStruct((b, s, BRANCH_WIDTH), BF16),
        scratch_shapes=[
            pltpu.VMEM((hp, s // t, hd, t), BF16),
            pltpu.VMEM((hp, s, 2 * hd), BF16),
            pltpu.VMEM((hp, 2, t, hd), BF16),
            pltpu.VMEM((hp, 2, t, LANES), F32),
            pltpu.VMEM((hp, 2, t, 2 * hd), F32),
        ],
        compiler_params=_params("parallel", "parallel", "arbitrary"),
        name="diff_attention",
    )(lamv, qkv, qkv, qkv, rest, bias, da_norm_g.reshape(-1, 1, BRANCH_WIDTH))


def _sb_kernel(q_ref, k_ref, v_ref, z_ref, o_ref, kt_scr, q_scr, c_scr, acc_scr, *, t, hp):
    qi = pl.program_id(2)
    nt = kt_scr.shape[1]
    hd = SB_HEAD_DIM
    cb = min(SB_CUMSUM_BLOCK, t)

    @pl.when(qi == 0)
    def _():
        for h in range(hp):
            for jj in range(nt):
                kt_scr[h, jj] = k_ref[0, jj * t:(jj + 1) * t, h * hd:(h + 1) * hd].astype(F32).T.astype(BF16)

    for h in range(hp):
        q_scr[h] = (q_ref[0, :, h * hd:(h + 1) * hd].astype(F32) * (SB_HEAD_DIM ** -0.5 * LOG2E)).astype(BF16)
    c_scr[...] = jnp.zeros(c_scr.shape, F32)
    acc_scr[...] = jnp.zeros(acc_scr.shape, F32)

    def tile(j, diag):
        off = pl.multiple_of(j * t, t)
        r2 = lax.broadcasted_iota(jnp.int32, (cb, cb), 0)
        c2 = lax.broadcasted_iota(jnp.int32, (cb, cb), 1)
        upper = jnp.where(r2 >= c2, 1.0, 0.0).astype(BF16)
        if diag:
            row = lax.broadcasted_iota(jnp.int32, (t, t), 0)
            col = lax.broadcasted_iota(jnp.int32, (t, t), 1)
            causal = col < row
        for h in range(hp):
            v = v_ref[0, pl.ds(off, t), h * hd:(h + 1) * hd]
            z2 = _dot(q_scr[h], kt_scr[h, j])
            w = jnp.maximum(z2, 0.0) + jnp.log(1.0 + jnp.exp2(_neg_abs(z2))) * (1.0 / LN2)
            if diag:
                w = jnp.where(causal, w, 0.0)
            c = c_scr[h]
            parts = []
            for blk in reversed(range(t // cb)):
                sl = slice(blk * cb, (blk + 1) * cb)
                cs = _dot(w[:, sl].astype(BF16), upper)
                parts.append((z2[:, sl] - _lane_tile(c, cb // LANES)) - cs)
                c = c + jnp.broadcast_to(cs[:, 0:1], c.shape)
            c_scr[h] = c
            a = jnp.exp2(jnp.concatenate(parts[::-1], axis=1))
            if diag:
                a = jnp.where(causal, a, 0.0)
            acc_scr[h] += _dot(a.astype(BF16), v)

    @pl.when(qi == 0)
    def _():
        tile(0, True)

    @pl.when(qi >= 1)
    def _():
        tile(qi, True)
        tile(qi - 1, False)

    n_rest = jnp.maximum(qi - 1, 0)

    def body(jp, carry):
        tile(qi - 2 - 2 * jp, False)
        tile(qi - 3 - 2 * jp, False)
        return carry

    lax.fori_loop(0, n_rest // 2, body, 0)

    @pl.when(n_rest % 2 == 1)
    def _():
        tile(0, False)

    for h in range(hp):
        o_ref[0, :, h * hd:(h + 1) * hd] = (acc_scr[h] * _silu(z_ref[0, :, h * hd:(h + 1) * hd])).astype(o_ref.dtype)


def stick_breaking_branch(qkv, rest, t, hp):
    b, s, _ = qkv.shape
    t = _tile(s, t)
    hd = SB_HEAD_DIM
    w = hp * hd
    cpb = BRANCH_WIDTH // w
    return pl.pallas_call(
        functools.partial(_sb_kernel, t=t, hp=hp),
        grid=(b, SB_HEADS // hp, s // t),
        in_specs=[
            pl.BlockSpec((1, t, w), lambda bi, h, qi: (bi, qi, QKV_SB_Q * cpb + h)),
            pl.BlockSpec((1, s, w), lambda bi, h, qi: (bi, 0, QKV_SB_K * cpb + h)),
            pl.BlockSpec((1, s, w), lambda bi, h, qi: (bi, 0, QKV_SB_V * cpb + h)),
            pl.BlockSpec((1, t, w), lambda bi, h, qi: (bi, qi, REST_SB_Z * cpb + h)),
        ],
        out_specs=pl.BlockSpec((1, t, w), lambda bi, h, qi: (bi, qi, h)),
        out_shape=jax.ShapeDtypeStruct((b, s, BRANCH_WIDTH), BF16),
        scratch_shapes=[
            pltpu.VMEM((hp, s // t, hd, t), BF16),
            pltpu.VMEM((hp, t, hd), BF16),
            pltpu.VMEM((hp, t, LANES), F32),
            pltpu.VMEM((hp, t, hd), F32),
        ],
        compiler_params=_params("parallel", "parallel", "arbitrary"),
        name="stick_breaking",
    )(qkv, qkv, qkv, rest)


def _pool_kernel(u_ref, z_ref, w_ref, sc_ref, o_ref, ext_scr, *, tt):
    ti = pl.program_id(1)

    @pl.when(ti == 0)
    def _():
        ext_scr[0:POOL_HALO] = jnp.zeros((POOL_HALO, BRANCH_WIDTH), F32)

    ext_scr[POOL_HALO:POOL_HALO + tt] = u_ref[0]
    pos = ti * tt + lax.broadcasted_iota(jnp.int32, (tt, 1), 0)
    for g, w in enumerate(POOL_WINDOWS):
        cols = slice(g * POOL_GROUP, (g + 1) * POOL_GROUP)
        u = ext_scr[POOL_HALO:POOL_HALO + tt, cols]
        win = u
        for j in range(1, w):
            win = win + ext_scr[POOL_HALO - j:POOL_HALO - j + tt, cols]
        count = jnp.minimum(pos + 1, w).astype(F32)
        pooled = win / count - u
        mixed = _dot(pooled.astype(BF16), w_ref[g]) * sc_ref[:, cols]
        o_ref[0, :, cols] = (mixed * _silu(z_ref[0, :, cols])).astype(o_ref.dtype)
    ext_scr[0:POOL_HALO] = ext_scr[tt:tt + POOL_HALO]


def pool_branch(rest, w_pool_bf16, pool_scale, layer, tt=512):
    b, s, _ = rest.shape
    tt = _tile(s, tt)
    assert tt >= POOL_HALO and max(POOL_WINDOWS) <= POOL_HALO
    return pl.pallas_call(
        functools.partial(_pool_kernel, tt=tt),
        grid=(b, s // tt),
        in_specs=[
            pl.BlockSpec((1, tt, BRANCH_WIDTH), lambda bi, ti: (bi, ti, REST_POOL_U)),
            pl.BlockSpec((1, tt, BRANCH_WIDTH), lambda bi, ti: (bi, ti, REST_POOL_Z)),
            pl.BlockSpec((None,) + w_pool_bf16.shape[1:], lambda bi, ti: (layer, 0, 0, 0)),
            pl.BlockSpec((None, 1, BRANCH_WIDTH), lambda bi, ti: (layer, 0, 0)),
        ],
        out_specs=pl.BlockSpec((1, tt, BRANCH_WIDTH), lambda bi, ti: (bi, ti, 0)),
        out_shape=jax.ShapeDtypeStruct((b, s, BRANCH_WIDTH), BF16),
        scratch_shapes=[pltpu.VMEM((tt + POOL_HALO, BRANCH_WIDTH), F32)],
        compiler_params=_params("parallel", "arbitrary"),
        name="multiscale_pool",
    )(rest, rest, w_pool_bf16, pool_scale.reshape(-1, 1, BRANCH_WIDTH))


def _mem_attn_kernel(q_ref, z_ref, mk_ref, mv_ref, o_ref):
    q = (q_ref[0] * (MEM_HEAD_DIM ** -0.5)).astype(BF16)
    s = _dot_nt(q, mk_ref[0].astype(BF16))
    p = jnp.exp(s - jnp.max(s, axis=-1, keepdims=True))
    l = jnp.sum(p, axis=-1, keepdims=True)
    o = _dot(p.astype(BF16), mv_ref[0].astype(BF16)) / l
    o_ref[0] = (o * _silu(z_ref[0])).astype(o_ref.dtype)


def memory_branch(qkv, rest, mkv, tq=1024):
    b, s, _ = qkv.shape
    m = mkv.shape[1]
    tq = _tile(s, tq)
    hd = MEM_HEAD_DIM
    cpb = BRANCH_WIDTH // hd
    return pl.pallas_call(
        _mem_attn_kernel,
        grid=(b, MEM_HEADS, s // tq),
        in_specs=[
            pl.BlockSpec((1, tq, hd), lambda bi, h, qi: (bi, qi, QKV_MEM_Q * cpb + h)),
            pl.BlockSpec((1, tq, hd), lambda bi, h, qi: (bi, qi, REST_MEM_Z * cpb + h)),
            pl.BlockSpec((1, m, hd), lambda bi, h, qi: (bi, 0, h)),
            pl.BlockSpec((1, m, hd), lambda bi, h, qi: (bi, 0, cpb + h)),
        ],
        out_specs=pl.BlockSpec((1, tq, hd), lambda bi, h, qi: (bi, qi, h)),
        out_shape=jax.ShapeDtypeStruct((b, s, BRANCH_WIDTH), BF16),
        compiler_params=_params("parallel", "parallel", "parallel"),
        name="memory_attention",
    )(qkv, rest, mkv, mkv)


def _merge_kernel(b0_ref, b1_ref, b2_ref, b3_ref, g0_ref, g1_ref, g2_ref, g3_ref, gb_ref, w_ref, o_ref):
    branches = (b0_ref, b1_ref, b2_ref, b3_ref)
    gates = (g0_ref, g1_ref, g2_ref, g3_ref)
    merged = None
    for n in range(N_BRANCH):
        gate = 1.0 / (1.0 + jnp.exp(-(gates[n][...] + gb_ref[n:n + 1, :])))
        term = gate * _dot(branches[n][...], w_ref[n])
        merged = term if merged is None else merged + term
    o_ref[...] = merged.astype(o_ref.dtype)


def gated_merge(branches, rest2d, gate_b, w_branch_bf16, layer, tm=512, tn=1024):
    m = rest2d.shape[0]
    d = w_branch_bf16.shape[3]
    tm, tn = _tile(m, tm), _tile(d, tn)
    gate_col0 = REST_GATES * BRANCH_WIDTH // tn
    per_gate = d // tn
    br_spec = pl.BlockSpec((tm, BRANCH_WIDTH), lambda j, i: (i, 0))

    def gate_spec(n):
        return pl.BlockSpec((tm, tn), lambda j, i: (i, gate_col0 + n * per_gate + j))

    return pl.pallas_call(
        _merge_kernel,
        grid=(d // tn, m // tm),
        in_specs=[br_spec] * N_BRANCH + [gate_spec(n) for n in range(N_BRANCH)] + [
            pl.BlockSpec((None, N_BRANCH, tn), lambda j, i: (layer, 0, j)),
            pl.BlockSpec((None, N_BRANCH, BRANCH_WIDTH, tn), lambda j, i: (layer, 0, 0, j)),
        ],
        out_specs=pl.BlockSpec((tm, tn), lambda j, i: (i, j)),
        out_shape=jax.ShapeDtypeStruct((m, d), BF16),
        compiler_params=_params("parallel", "parallel"),
        name="gated_merge",
    )(*branches, rest2d, rest2d, rest2d, rest2d, gate_b, w_branch_bf16)


def kernel(x, mem, rel_bias, norm_g, w_in, gate_b, lam_q1, lam_k1, lam_q2, lam_k2, da_norm_g, w_pool,
           pool_scale, mem_norm_g, w_mem_kv, w_branch, w_out, final_g):
    b, s, d = x.shape
    depth = norm_g.shape[0]
    n_mem = mem.shape[1]
    t = min(ATTN_TILE, s)
    hp = ATTN_HEADS_PER_STEP

    w_pool_bf, w_mem_kv_bf = w_pool.astype(BF16), w_mem_kv.astype(BF16)
    w_branch_bf, w_out_bf = w_branch.astype(BF16), w_out.astype(BF16)
    lamv = jnp.stack([lam_q1, lam_k1, lam_q2, lam_k2], axis=1)
    n_gate_blocks = (w_in.shape[2] - N_SLICES * BRANCH_WIDTH) // BRANCH_WIDTH
    rest_blocks = REST_SLICES + tuple(range(N_SLICES, N_SLICES + n_gate_blocks))
    mkv_blocks = tuple(range(w_mem_kv.shape[2] // BRANCH_WIDTH))

    bias = bias_tiles(rel_bias, t)
    x2d = x.reshape(b * s, d)
    mem2d = mem.reshape(b * n_mem, d)
    for l in range(depth):
        lam_init = 0.8 - 0.6 * math.exp(-0.3 * l)
        qkv = norm_matmul(x2d, norm_g, w_in, l, QKV_SLICES, BF16).reshape(b, s, -1)
        rest2d = norm_matmul(x2d, norm_g, w_in, l, rest_blocks, F32)
        rest = rest2d.reshape(b, s, -1)

        br_da = diff_attention_branch(qkv, rest, bias, lamv, da_norm_g, l, lam_init, t, hp)
        br_sb = stick_breaking_branch(qkv, rest, t, hp)
        br_pool = pool_branch(rest, w_pool_bf, pool_scale, l)
        mkv = norm_matmul(mem2d, mem_norm_g, w_mem_kv_bf, l, mkv_blocks, F32)
        br_mem = memory_branch(qkv, rest, mkv.reshape(b, n_mem, 2 * BRANCH_WIDTH))

        branches = [br.reshape(b * s, BRANCH_WIDTH) for br in (br_da, br_sb, br_pool, br_mem)]
        merged = gated_merge(branches, rest2d, gate_b, w_branch_bf, l)
        x2d = matmul_residual(merged, w_out_bf, l, x2d, final_g, final_norm=(l == depth - 1))
    return x2d.reshape(b, s, d)
```

```python
import functools
import math

import jax
import jax.numpy as jnp
from jax import lax
from jax.experimental import pallas as pl
from jax.experimental.pallas import tpu as pltpu

F32 = jnp.float32
BF16 = jnp.bfloat16

BRANCH_WIDTH = 1024
N_BRANCH = 4
N_SLICES = 12
DA_HEADS = 8
DA_QK_DIM = 64
DA_V_DIM = 2 * DA_QK_DIM
SB_HEADS = 8
SB_HEAD_DIM = BRANCH_WIDTH // SB_HEADS
POOL_WINDOWS = (2, 4, 8, 16)
POOL_GROUP = BRANCH_WIDTH // len(POOL_WINDOWS)
MEM_HEADS = 4
MEM_HEAD_DIM = BRANCH_WIDTH // MEM_HEADS
REL_BUCKETS = 32
REL_MAX_DIST = 128
EPS = 1e-6

SL_DA_Q, SL_DA_K, SL_DA_V, SL_DA_Z = 0, 1, 2, 3
SL_SB_Q, SL_SB_K, SL_SB_V, SL_SB_Z = 4, 5, 6, 7
SL_POOL_U, SL_POOL_Z, SL_MEM_Q, SL_MEM_Z = 8, 9, 10, 11
QKV_SLICES = (SL_DA_Q, SL_DA_K, SL_DA_V, SL_SB_Q, SL_SB_K, SL_SB_V, SL_MEM_Q)
REST_SLICES = (SL_DA_Z, SL_SB_Z, SL_POOL_U, SL_POOL_Z, SL_MEM_Z)
QKV_DA_Q, QKV_DA_K, QKV_DA_V, QKV_SB_Q, QKV_SB_K, QKV_SB_V, QKV_MEM_Q = range(len(QKV_SLICES))
REST_DA_Z, REST_SB_Z, REST_POOL_U, REST_POOL_Z, REST_MEM_Z, REST_GATES = range(len(REST_SLICES) + 1)

V7X_VMEM_LIMIT_BYTES = 56 * 1024 * 1024
LANES = 128
POOL_HALO = 16
NEG_BIG = -1e30
ATTN_TILE = 512
ATTN_HEADS_PER_STEP = 2
SB_CUMSUM_BLOCK = 256
LOG2E = 1.4426950408889634
LN2 = 0.6931471805599453


def _params(*sem):
    return pltpu.CompilerParams(dimension_semantics=sem, vmem_limit_bytes=V7X_VMEM_LIMIT_BYTES)


def _tile(n, t):
    t = min(t, n)
    assert n % t == 0, (n, t)
    return t


def _silu(z):
    return z * (1.0 / (1.0 + jnp.exp(-z)))


def _dot_nt(a, b):
    return lax.dot_general(a, b, (((1,), (1,)), ((), ())), preferred_element_type=F32)


def _dot(a, b):
    return jnp.dot(a, b, preferred_element_type=F32)


def _lane_tile(x, n):
    return x if n == 1 else jnp.concatenate([x] * n, axis=1)


def _neg_abs(x):
    u = lax.bitcast_convert_type(x, jnp.uint32) | jnp.uint32(0x80000000)
    return lax.bitcast_convert_type(u, F32)


def _norm_matmul_kernel(x_ref, g_ref, w_ref, o_ref, h_scr):
    @pl.when(pl.program_id(1) == 0)
    def _():
        x = x_ref[...]
        y = x * lax.rsqrt(jnp.mean(x * x, axis=-1, keepdims=True) + EPS)
        h_scr[...] = (y * g_ref[...]).astype(BF16)

    o_ref[...] = _dot(h_scr[...], w_ref[...]).astype(o_ref.dtype)


def _static_lookup(table, j):
    r = jnp.int32(table[0])
    for idx in range(1, len(table)):
        r = jnp.where(j == idx, jnp.int32(table[idx]), r)
    return r


def norm_matmul(x2d, gains, w, layer, col_blocks, out_dtype, tm=1024, tn=BRANCH_WIDTH):
    m, d = x2d.shape
    tm = _tile(m, tm)
    assert w.shape[2] % tn == 0
    n_out = len(col_blocks)
    return pl.pallas_call(
        _norm_matmul_kernel,
        grid=(m // tm, n_out),
        in_specs=[
            pl.BlockSpec((tm, d), lambda i, j: (i, 0)),
            pl.BlockSpec((None, 1, d), lambda i, j: (layer, 0, 0)),
            pl.BlockSpec((None, d, tn), lambda i, j: (layer, 0, _static_lookup(col_blocks, j))),
        ],
        out_specs=pl.BlockSpec((tm, tn), lambda i, j: (i, j)),
        out_shape=jax.ShapeDtypeStruct((m, n_out * tn), out_dtype),
        scratch_shapes=[pltpu.VMEM((tm, d), BF16)],
        compiler_params=_params("parallel", "arbitrary"),
        name="norm_matmul",
    )(x2d, gains.reshape(-1, 1, d), w)


def _matmul_residual_kernel(a_ref, w_ref, r_ref, g_ref, o_ref, *, final_norm):
    y = r_ref[...] + _dot(a_ref[...], w_ref[...])
    if final_norm:
        y = (y * lax.rsqrt(jnp.mean(y * y, axis=-1, keepdims=True) + EPS)) * g_ref[...]
    o_ref[...] = y


def matmul_residual(a_bf16, w_bf16, layer, res, norm_g, final_norm, tm=512):
    m, k = a_bf16.shape
    n = w_bf16.shape[2]
    tm = _tile(m, tm)
    g = norm_g.reshape(1, n)
    return pl.pallas_call(
        functools.partial(_matmul_residual_kernel, final_norm=final_norm),
        grid=(m // tm,),
        in_specs=[
            pl.BlockSpec((tm, k), lambda i: (i, 0)),
            pl.BlockSpec((None, k, n), lambda i: (layer, 0, 0)),
            pl.BlockSpec((tm, n), lambda i: (i, 0)),
            pl.BlockSpec((1, n), lambda i: (0, 0)),
        ],
        out_specs=pl.BlockSpec((tm, n), lambda i: (i, 0)),
        out_shape=jax.ShapeDtypeStruct((m, n), F32),
        compiler_params=_params("parallel"),
        name="out_proj_residual",
    )(a_bf16, w_bf16, res, g)


def _bias_tiles_kernel(rb_ref, o_ref, *, t):
    h = pl.program_id(0)
    qi = lax.broadcasted_iota(jnp.int32, (t, t), 0)
    ki = lax.broadcasted_iota(jnp.int32, (t, t), 1)
    max_exact = REL_BUCKETS // 2
    far = rb_ref[REL_BUCKETS - 1, h]
    for sel in range(2):
        rel = qi - ki + sel * t
        n = jnp.maximum(rel, 0)
        nf = jnp.maximum(n, 1).astype(F32)
        large = max_exact + (jnp.log(nf / max_exact) / math.log(REL_MAX_DIST / max_exact)
                             * (REL_BUCKETS - max_exact)).astype(jnp.int32)
        large = jnp.minimum(large, REL_BUCKETS - 1)
        bucket = jnp.where(n < max_exact, n, large)
        val = jnp.zeros((t, t), F32)
        for b in range(REL_BUCKETS - 1):
            val = jnp.where(bucket == b, rb_ref[b, h] - far, val)
        if sel == 0:
            val = jnp.where(rel >= 0, val, NEG_BIG)
        o_ref[0, sel] = val


def bias_tiles(rel_bias, t):
    assert t >= REL_MAX_DIST
    heads = rel_bias.shape[1]
    return pl.pallas_call(
        functools.partial(_bias_tiles_kernel, t=t),
        grid=(heads,),
        in_specs=[pl.BlockSpec(memory_space=pltpu.SMEM)],
        out_specs=pl.BlockSpec((1, 2, t, t), lambda h: (h, 0, 0, 0)),
        out_shape=jax.ShapeDtypeStruct((heads, 2, t, t), F32),
        compiler_params=_params("parallel"),
        name="t5_bias_tiles",
    )(rel_bias)


def _da_kernel(lamv_ref, q_ref, k_ref, v_ref, z_ref, bias_ref, g_ref, o_ref,
               kt_scr, v_scr, q_scr, m_scr, acc_scr, *, t, hp, lam_init):
    qi = pl.program_id(2)
    nt = kt_scr.shape[1]
    hd = DA_V_DIM

    @pl.when(qi == 0)
    def _():
        for h in range(hp):
            for jj in range(nt):
                kt_scr[h, jj] = k_ref[0, jj * t:(jj + 1) * t, h * hd:(h + 1) * hd].astype(F32).T.astype(BF16)
            v_scr[h, :, 0:hd] = v_ref[0, :, h * hd:(h + 1) * hd]
            v_scr[h, :, hd:2 * hd] = jnp.ones((v_scr.shape[1], hd), BF16)

    lane = lax.broadcasted_iota(jnp.int32, (t, hd), 1)
    for h in range(hp):
        q = q_ref[0, :, h * hd:(h + 1) * hd] * (DA_QK_DIM ** -0.5)
        q_scr[h, 0] = jnp.where(lane < DA_QK_DIM, q, 0.0).astype(BF16)
        q_scr[h, 1] = jnp.where(lane >= DA_QK_DIM, q, 0.0).astype(BF16)
    m_scr[...] = jnp.full(m_scr.shape, -jnp.inf, F32)
    acc_scr[...] = jnp.zeros(acc_scr.shape, F32)

    def tile(j, bias_sel):
        off = pl.multiple_of(j * t, t)
        for h in range(hp):
            kt = kt_scr[h, j]
            v = v_scr[h, pl.ds(off, t), :]
            for c in range(2):
                s = _dot(q_scr[h, c], kt)
                if bias_sel is not None:
                    s = s + bias_ref[h, bias_sel]
                m_prev = m_scr[h, c]
                m_new = jnp.maximum(m_prev, jnp.max(s, axis=-1, keepdims=True))
                alpha = jnp.exp(m_prev - m_new)
                p = jnp.exp(s - _lane_tile(m_new, t // LANES))
                acc_scr[h, c] = _lane_tile(alpha, 2) * acc_scr[h, c] + _dot(p.astype(BF16), v)
                m_scr[h, c] = m_new

    n_far = jnp.maximum(qi - 1, 0)

    def far_body(jp, carry):
        tile(2 * jp, None)
        tile(2 * jp + 1, None)
        return carry

    lax.fori_loop(0, n_far // 2, far_body, 0)

    @pl.when(n_far % 2 == 1)
    def _():
        tile(n_far - 1, None)
        tile(qi - 1, 1)
        tile(qi, 0)

    @pl.when((qi >= 1) & (n_far % 2 == 0))
    def _():
        tile(qi - 1, 1)
        tile(qi, 0)

    @pl.when(qi == 0)
    def _():
        tile(0, 0)

    lv = lamv_ref[...]
    s1 = jnp.sum(lv[0:1] * lv[1:2], axis=-1, keepdims=True)
    s2 = jnp.sum(lv[2:3] * lv[3:4], axis=-1, keepdims=True)
    lam = jnp.exp(s1) - jnp.exp(s2) + lam_init
    for h in range(hp):
        a0 = acc_scr[h, 0]
        a1 = acc_scr[h, 1]
        o = a0[:, 0:hd] / a0[:, hd:2 * hd] - lam * (a1[:, 0:hd] / a1[:, hd:2 * hd])
        y = o * lax.rsqrt(jnp.mean(o * o, axis=-1, keepdims=True) + EPS)
        y = (y * g_ref[:, h * hd:(h + 1) * hd]) * (1.0 - lam_init)
        o_ref[0, :, h * hd:(h + 1) * hd] = (y * _silu(z_ref[0, :, h * hd:(h + 1) * hd])).astype(o_ref.dtype)


def diff_attention_branch(qkv, rest, bias, lamv, da_norm_g, layer, lam_init, t, hp):
    b, s, _ = qkv.shape
    t = _tile(s, t)
    hd = DA_V_DIM
    w = hp * hd
    cpb = BRANCH_WIDTH // w
    return pl.pallas_call(
        functools.partial(_da_kernel, t=t, hp=hp, lam_init=lam_init),
        grid=(b, DA_HEADS // hp, s // t),
        in_specs=[
            pl.BlockSpec((None, 4, DA_QK_DIM), lambda bi, h, qi: (layer, 0, 0)),
            pl.BlockSpec((1, t, w), lambda bi, h, qi: (bi, qi, QKV_DA_Q * cpb + h)),
            pl.BlockSpec((1, s, w), lambda bi, h, qi: (bi, 0, QKV_DA_K * cpb + h)),
            pl.BlockSpec((1, s, w), lambda bi, h, qi: (bi, 0, QKV_DA_V * cpb + h)),
            pl.BlockSpec((1, t, w), lambda bi, h, qi: (bi, qi, REST_DA_Z * cpb + h)),
            pl.BlockSpec((hp, 2, t, t), lambda bi, h, qi: (h, 0, 0, 0)),
            pl.BlockSpec((None, 1, w), lambda bi, h, qi: (layer, 0, h)),
        ],
        out_specs=pl.BlockSpec((1, t, w), lambda bi, h, qi: (bi, qi, h)),
        out_shape=jax.ShapeDtypeStruct((b, s, BRANCH_WIDTH), BF16),
        scratch_shapes=[
            pltpu.VMEM((hp, s // t, hd, t), BF16),
            pltpu.VMEM((hp, s, 2 * hd), BF16),
            pltpu.VMEM((hp, 2, t, hd), BF16),
            pltpu.VMEM((hp, 2, t, LANES), F32),
            pltpu.VMEM((hp, 2, t, 2 * hd), F32),
        ],
        compiler_params=_params("parallel", "parallel", "arbitrary"),
        name="diff_attention",
    )(lamv, qkv, qkv, qkv, rest, bias, da_norm_g.reshape(-1, 1, BRANCH_WIDTH))


def _sb_kernel(q_ref, k_ref, v_ref, z_ref, o_ref, kt_scr, q_scr, c_scr, acc_scr, *, t, hp):
    qi = pl.program_id(2)
    nt = kt_scr.shape[1]
    hd = SB_HEAD_DIM
    cb = min(SB_CUMSUM_BLOCK, t)

    @pl.when(qi == 0)
    def _():
        for h in range(hp):
            for jj in range(nt):
                kt_scr[h, jj] = k_ref[0, jj * t:(jj + 1) * t, h * hd:(h + 1) * hd].astype(F32).T.astype(BF16)

    for h in range(hp):
        q_scr[h] = (q_ref[0, :, h * hd:(h + 1) * hd].astype(F32) * (SB_HEAD_DIM ** -0.5 * LOG2E)).astype(BF16)
    c_scr[...] = jnp.zeros(c_scr.shape, F32)
    acc_scr[...] = jnp.zeros(acc_scr.shape, F32)

    def tile(j, diag):
        off = pl.multiple_of(j * t, t)
        r2 = lax.broadcasted_iota(jnp.int32, (cb, cb), 0)
        c2 = lax.broadcasted_iota(jnp.int32, (cb, cb), 1)
        upper = jnp.where(r2 >= c2, 1.0, 0.0).astype(BF16)
        if diag:
            row = lax.broadcasted_iota(jnp.int32, (t, t), 0)
            col = lax.broadcasted_iota(jnp.int32, (t, t), 1)
            causal = col < row
        for h in range(hp):
            v = v_ref[0, pl.ds(off, t), h * hd:(h + 1) * hd]
            z2 = _dot(q_scr[h], kt_scr[h, j])
            w = jnp.maximum(z2, 0.0) + jnp.log(1.0 + jnp.exp2(_neg_abs(z2))) * (1.0 / LN2)
            if diag:
                w = jnp.where(causal, w, 0.0)
            c = c_scr[h]
            parts = []
            for blk in reversed(range(t // cb)):
                sl = slice(blk * cb, (blk + 1) * cb)
                cs = _dot(w[:, sl].astype(BF16), upper)
                parts.append((z2[:, sl] - _lane_tile(c, cb // LANES)) - cs)
                c = c + jnp.broadcast_to(cs[:, 0:1], c.shape)
            c_scr[h] = c
            a = jnp.exp2(jnp.concatenate(parts[::-1], axis=1))
            if diag:
                a = jnp.where(causal, a, 0.0)
            acc_scr[h] += _dot(a.astype(BF16), v)

    n_rest = jnp.maximum(qi - 1, 0)
    odd = n_rest % 2

    @pl.when(qi == 0)
    def _():
        tile(0, True)

    @pl.when((qi >= 1) & (odd == 0))
    def _():
        tile(qi, True)
        tile(qi - 1, False)

    @pl.when(odd == 1)
    def _():
        tile(qi, True)
        tile(qi - 1, False)
        tile(qi - 2, False)

    def body(jp, carry):
        tile(qi - 2 - odd - 2 * jp, False)
        tile(qi - 3 - odd - 2 * jp, False)
        return carry

    lax.fori_loop(0, n_rest // 2, body, 0)

    for h in range(hp):
        o_ref[0, :, h * hd:(h + 1) * hd] = (acc_scr[h] * _silu(z_ref[0, :, h * hd:(h + 1) * hd])).astype(o_ref.dtype)


def stick_breaking_branch(qkv, rest, t, hp):
    b, s, _ = qkv.shape
    t = _tile(s, t)
    hd = SB_HEAD_DIM
    w = hp * hd
    cpb = BRANCH_WIDTH // w
    return pl.pallas_call(
        functools.partial(_sb_kernel, t=t, hp=hp),
        grid=(b, SB_HEADS // hp, s // t),
        in_specs=[
            pl.BlockSpec((1, t, w), lambda bi, h, qi: (bi, qi, QKV_SB_Q * cpb + h)),
            pl.BlockSpec((1, s, w), lambda bi, h, qi: (bi, 0, QKV_SB_K * cpb + h)),
            pl.BlockSpec((1, s, w), lambda bi, h, qi: (bi, 0, QKV_SB_V * cpb + h)),
            pl.BlockSpec((1, t, w), lambda bi, h, qi: (bi, qi, REST_SB_Z * cpb + h)),
        ],
        out_specs=pl.BlockSpec((1, t, w), lambda bi, h, qi: (bi, qi, h)),
        out_shape=jax.ShapeDtypeStruct((b, s, BRANCH_WIDTH), BF16),
        scratch_shapes=[
            pltpu.VMEM((hp, s // t, hd, t), BF16),
            pltpu.VMEM((hp, t, hd), BF16),
            pltpu.VMEM((hp, t, LANES), F32),
            pltpu.VMEM((hp, t, hd), F32),
        ],
        compiler_params=_params("parallel", "parallel", "arbitrary"),
        name="stick_breaking",
    )(qkv, qkv, qkv, rest)


def _pool_kernel(u_ref, z_ref, w_ref, sc_ref, o_ref, ext_scr, *, tt):
    ti = pl.program_id(1)

    @pl.when(ti == 0)
    def _():
        ext_scr[0:POOL_HALO] = jnp.zeros((POOL_HALO, BRANCH_WIDTH), F32)

    ext_scr[POOL_HALO:POOL_HALO + tt] = u_ref[0]
    pos = ti * tt + lax.broadcasted_iota(jnp.int32, (tt, 1), 0)
    for g, w in enumerate(POOL_WINDOWS):
        cols = slice(g * POOL_GROUP, (g + 1) * POOL_GROUP)
        u = ext_scr[POOL_HALO:POOL_HALO + tt, cols]
        win = u
        for j in range(1, w):
            win = win + ext_scr[POOL_HALO - j:POOL_HALO - j + tt, cols]
        count = jnp.minimum(pos + 1, w).astype(F32)
        pooled = win / count - u
        mixed = _dot(pooled.astype(BF16), w_ref[g]) * sc_ref[:, cols]
        o_ref[0, :, cols] = (mixed * _silu(z_ref[0, :, cols])).astype(o_ref.dtype)
    ext_scr[0:POOL_HALO] = ext_scr[tt:tt + POOL_HALO]


def pool_branch(rest, w_pool_bf16, pool_scale, layer, tt=512):
    b, s, _ = rest.shape
    tt = _tile(s, tt)
    assert tt >= POOL_HALO and max(POOL_WINDOWS) <= POOL_HALO
    return pl.pallas_call(
        functools.partial(_pool_kernel, tt=tt),
        grid=(b, s // tt),
        in_specs=[
            pl.BlockSpec((1, tt, BRANCH_WIDTH), lambda bi, ti: (bi, ti, REST_POOL_U)),
            pl.BlockSpec((1, tt, BRANCH_WIDTH), lambda bi, ti: (bi, ti, REST_POOL_Z)),
            pl.BlockSpec((None,) + w_pool_bf16.shape[1:], lambda bi, ti: (layer, 0, 0, 0)),
            pl.BlockSpec((None, 1, BRANCH_WIDTH), lambda bi, ti: (layer, 0, 0)),
        ],
        out_specs=pl.BlockSpec((1, tt, BRANCH_WIDTH), lambda bi, ti: (bi, ti, 0)),
        out_shape=jax.ShapeDtypeStruct((b, s, BRANCH_WIDTH), BF16),
        scratch_shapes=[pltpu.VMEM((tt + POOL_HALO, BRANCH_WIDTH), F32)],
        compiler_params=_params("parallel", "arbitrary"),
        name="multiscale_pool",
    )(rest, rest, w_pool_bf16, pool_scale.reshape(-1, 1, BRANCH_WIDTH))


def _mem_attn_kernel(q_ref, z_ref, mk_ref, mv_ref, o_ref):
    hd = MEM_HEAD_DIM
    for h in range(MEM_HEADS):
        cols = slice(h * hd, (h + 1) * hd)
        q = q_ref[0, :, cols] * (MEM_HEAD_DIM ** -0.5)
        s = _dot_nt(q, mk_ref[0, :, cols].astype(BF16))
        p = jnp.exp(s - jnp.max(s, axis=-1, keepdims=True))
        l = jnp.sum(p, axis=-1, keepdims=True)
        o = _dot(p.astype(BF16), mv_ref[0, :, cols].astype(BF16)) / l
        o_ref[0, :, cols] = (o * _silu(z_ref[0, :, cols])).astype(o_ref.dtype)


def memory_branch(qkv, rest, mkv, tq=1024):
    b, s, _ = qkv.shape
    m = mkv.shape[1]
    tq = _tile(s, tq)
    w = BRANCH_WIDTH
    return pl.pallas_call(
        _mem_attn_kernel,
        grid=(b, s // tq),
        in_specs=[
            pl.BlockSpec((1, tq, w), lambda bi, qi: (bi, qi, QKV_MEM_Q)),
            pl.BlockSpec((1, tq, w), lambda bi, qi: (bi, qi, REST_MEM_Z)),
            pl.BlockSpec((1, m, w), lambda bi, qi: (bi, 0, 0)),
            pl.BlockSpec((1, m, w), lambda bi, qi: (bi, 0, 1)),
        ],
        out_specs=pl.BlockSpec((1, tq, w), lambda bi, qi: (bi, qi, 0)),
        out_shape=jax.ShapeDtypeStruct((b, s, BRANCH_WIDTH), BF16),
        compiler_params=_params("parallel", "parallel"),
        name="memory_attention",
    )(qkv, rest, mkv, mkv)


def _merge_kernel(b0_ref, b1_ref, b2_ref, b3_ref, g0_ref, g1_ref, g2_ref, g3_ref, gb_ref, w_ref, o_ref):
    branches = (b0_ref, b1_ref, b2_ref, b3_ref)
    gates = (g0_ref, g1_ref, g2_ref, g3_ref)
    merged = None
    for n in range(N_BRANCH):
        gate = 1.0 / (1.0 + jnp.exp(-(gates[n][...] + gb_ref[n:n + 1, :])))
        term = gate * _dot(branches[n][...], w_ref[n])
        merged = term if merged is None else merged + term
    o_ref[...] = merged.astype(o_ref.dtype)


def gated_merge(branches, rest2d, gate_b, w_branch_bf16, layer, tm=512, tn=1024):
    m = rest2d.shape[0]
    d = w_branch_bf16.shape[3]
    tm, tn = _tile(m, tm), _tile(d, tn)
    gate_col0 = REST_GATES * BRANCH_WIDTH // tn
    per_gate = d // tn
    br_spec = pl.BlockSpec((tm, BRANCH_WIDTH), lambda j, i: (i, 0))

    def gate_spec(n):
        return pl.BlockSpec((tm, tn), lambda j, i: (i, gate_col0 + n * per_gate + j))

    return pl.pallas_call(
        _merge_kernel,
        grid=(d // tn, m // tm),
        in_specs=[br_spec] * N_BRANCH + [gate_spec(n) for n in range(N_BRANCH)] + [
            pl.BlockSpec((None, N_BRANCH, tn), lambda j, i: (layer, 0, j)),
            pl.BlockSpec((None, N_BRANCH, BRANCH_WIDTH, tn), lambda j, i: (layer, 0, 0, j)),
        ],
        out_specs=pl.BlockSpec((tm, tn), lambda j, i: (i, j)),
        out_shape=jax.ShapeDtypeStruct((m, d), BF16),
        compiler_params=_params("parallel", "parallel"),
        name="gated_merge",
    )(*branches, rest2d, rest2d, rest2d, rest2d, gate_b, w_branch_bf16)


def kernel(x, mem, rel_bias, norm_g, w_in, gate_b, lam_q1, lam_k1, lam_q2, lam_k2, da_norm_g, w_pool,
           pool_scale, mem_norm_g, w_mem_kv, w_branch, w_out, final_g):
    b, s, d = x.shape
    depth = norm_g.shape[0]
    n_mem = mem.shape[1]
    t = min(ATTN_TILE, s)
    hp = ATTN_HEADS_PER_STEP

    w_in_bf, w_pool_bf, w_mem_kv_bf = w_in.astype(BF16), w_pool.astype(BF16), w_mem_kv.astype(BF16)
    w_branch_bf, w_out_bf = w_branch.astype(BF16), w_out.astype(BF16)
    lamv = jnp.stack([lam_q1, lam_k1, lam_q2, lam_k2], axis=1)
    n_gate_blocks = (w_in.shape[2] - N_SLICES * BRANCH_WIDTH) // BRANCH_WIDTH
    rest_blocks = REST_SLICES + tuple(range(N_SLICES, N_SLICES + n_gate_blocks))
    mkv_blocks = tuple(range(w_mem_kv.shape[2] // BRANCH_WIDTH))

    bias = bias_tiles(rel_bias, t)
    x2d = x.reshape(b * s, d)
    mem2d = mem.reshape(b * n_mem, d)
    for l in range(depth):
        lam_init = 0.8 - 0.6 * math.exp(-0.3 * l)
        qkv = norm_matmul(x2d, norm_g, w_in_bf, l, QKV_SLICES, BF16).reshape(b, s, -1)
        rest2d = norm_matmul(x2d, norm_g, w_in_bf, l, rest_blocks, F32)
        rest = rest2d.reshape(b, s, -1)

        br_da = diff_attention_branch(qkv, rest, bias, lamv, da_norm_g, l, lam_init, t, hp)
        br_sb = stick_breaking_branch(qkv, rest, t, hp)
        br_pool = pool_branch(rest, w_pool_bf, pool_scale, l)
        mkv = norm_matmul(mem2d, mem_norm_g, w_mem_kv_bf, l, mkv_blocks, F32)
        br_mem = memory_branch(qkv, rest, mkv.reshape(b, n_mem, 2 * BRANCH_WIDTH))

        branches = [br.reshape(b * s, BRANCH_WIDTH) for br in (br_da, br_sb, br_pool, br_mem)]
        merged = gated_merge(branches, rest2d, gate_b, w_branch_bf, l)
        x2d = matmul_residual(merged, w_out_bf, l, x2d, final_g, final_norm=(l == depth - 1))
    return x2d.reshape(b, s, d)
```

```python
import functools
import math

import jax
import jax.numpy as jnp
from jax import lax
from jax.experimental import pallas as pl
from jax.experimental.pallas import tpu as pltpu

F32 = jnp.float32
BF16 = jnp.bfloat16

BRANCH_WIDTH = 1024
N_BRANCH = 4
N_SLICES = 12
DA_HEADS = 8
DA_QK_DIM = 64
DA_V_DIM = 2 * DA_QK_DIM
SB_HEADS = 8
SB_HEAD_DIM = BRANCH_WIDTH // SB_HEADS
POOL_WINDOWS = (2, 4, 8, 16)
POOL_GROUP = BRANCH_WIDTH // len(POOL_WINDOWS)
MEM_HEADS = 4
MEM_HEAD_DIM = BRANCH_WIDTH // MEM_HEADS
REL_BUCKETS = 32
REL_MAX_DIST = 128
EPS = 1e-6

SL_DA_Q, SL_DA_K, SL_DA_V, SL_DA_Z = 0, 1, 2, 3
SL_SB_Q, SL_SB_K, SL_SB_V, SL_SB_Z = 4, 5, 6, 7
SL_POOL_U, SL_POOL_Z, SL_MEM_Q, SL_MEM_Z = 8, 9, 10, 11
QKV_SLICES = (SL_DA_Q, SL_DA_K, SL_DA_V, SL_SB_Q, SL_SB_K, SL_SB_V, SL_MEM_Q)
REST_SLICES = (SL_DA_Z, SL_SB_Z, SL_POOL_U, SL_POOL_Z, SL_MEM_Z)
QKV_DA_Q, QKV_DA_K, QKV_DA_V, QKV_SB_Q, QKV_SB_K, QKV_SB_V, QKV_MEM_Q = range(len(QKV_SLICES))
REST_DA_Z, REST_SB_Z, REST_POOL_U, REST_POOL_Z, REST_MEM_Z, REST_GATES = range(len(REST_SLICES) + 1)

V7X_VMEM_LIMIT_BYTES = 56 * 1024 * 1024
LANES = 128
POOL_HALO = 16
NEG_BIG = -1e30
ATTN_TILE = 512
ATTN_HEADS_PER_STEP = 2
SB_CUMSUM_BLOCK = 256
LOG2E = 1.4426950408889634
LN2 = 0.6931471805599453


def _params(*sem):
    return pltpu.CompilerParams(dimension_semantics=sem, vmem_limit_bytes=V7X_VMEM_LIMIT_BYTES)


def _tile(n, t):
    t = min(t, n)
    assert n % t == 0, (n, t)
    return t


def _silu(z):
    return z * (1.0 / (1.0 + jnp.exp(-z)))


def _dot_nt(a, b):
    return lax.dot_general(a, b, (((1,), (1,)), ((), ())), preferred_element_type=F32)


def _dot(a, b):
    return jnp.dot(a, b, preferred_element_type=F32)


def _lane_tile(x, n):
    return x if n == 1 else jnp.concatenate([x] * n, axis=1)


def _neg_abs(x):
    u = lax.bitcast_convert_type(x, jnp.uint32) | jnp.uint32(0x80000000)
    return lax.bitcast_convert_type(u, F32)


def _rmsnorm_kernel(x_ref, g_ref, o_ref):
    x = x_ref[...]
    y = x * lax.rsqrt(jnp.mean(x * x, axis=-1, keepdims=True) + EPS)
    o_ref[...] = (y * g_ref[...]).astype(o_ref.dtype)


def rmsnorm_bf16(x2d, gains, layer, tr=512):
    m, d = x2d.shape
    tr = _tile(m, tr)
    return pl.pallas_call(
        _rmsnorm_kernel,
        grid=(m // tr,),
        in_specs=[
            pl.BlockSpec((tr, d), lambda i: (i, 0)),
            pl.BlockSpec((None, 1, d), lambda i: (layer, 0, 0)),
        ],
        out_specs=pl.BlockSpec((tr, d), lambda i: (i, 0)),
        out_shape=jax.ShapeDtypeStruct((m, d), BF16),
        compiler_params=_params("parallel"),
        name="rmsnorm",
    )(x2d, gains.reshape(-1, 1, d))


def _matmul_kernel(h_ref, w_ref, o_ref):
    o_ref[...] = _dot(h_ref[...], w_ref[...]).astype(o_ref.dtype)


def _static_lookup(table, j):
    r = jnp.int32(table[0])
    for idx in range(1, len(table)):
        r = jnp.where(j == idx, jnp.int32(table[idx]), r)
    return r


def matmul_cols(h, w, layer, col_blocks, out_dtype, tm=2048, tn=BRANCH_WIDTH):
    m, d = h.shape
    tm = _tile(m, tm)
    assert w.shape[2] % tn == 0
    n_out = len(col_blocks)
    return pl.pallas_call(
        _matmul_kernel,
        grid=(m // tm, n_out),
        in_specs=[
            pl.BlockSpec((tm, d), lambda i, j: (i, 0)),
            pl.BlockSpec((None, d, tn), lambda i, j: (layer, 0, _static_lookup(col_blocks, j))),
        ],
        out_specs=pl.BlockSpec((tm, tn), lambda i, j: (i, j)),
        out_shape=jax.ShapeDtypeStruct((m, n_out * tn), out_dtype),
        compiler_params=_params("parallel", "parallel"),
        name="in_proj",
    )(h, w)


def _matmul_residual_kernel(a_ref, w_ref, r_ref, g_ref, *out_refs, final_norm):
    y = r_ref[...] + _dot(a_ref[...], w_ref[...])
    normed = (y * lax.rsqrt(jnp.mean(y * y, axis=-1, keepdims=True) + EPS)) * g_ref[...]
    if final_norm:
        out_refs[0][...] = normed
    else:
        out_refs[0][...] = y
        out_refs[1][...] = normed.astype(BF16)


def matmul_residual(a_bf16, w_bf16, layer, res, gains, gain_index, final_norm, tm=512):
    m, k = a_bf16.shape
    n = w_bf16.shape[2]
    tm = _tile(m, tm)
    row_spec = pl.BlockSpec((tm, n), lambda i: (i, 0))
    out_shapes = [jax.ShapeDtypeStruct((m, n), F32)]
    if not final_norm:
        out_shapes.append(jax.ShapeDtypeStruct((m, n), BF16))
    return pl.pallas_call(
        functools.partial(_matmul_residual_kernel, final_norm=final_norm),
        grid=(m // tm,),
        in_specs=[
            pl.BlockSpec((tm, k), lambda i: (i, 0)),
            pl.BlockSpec((None, k, n), lambda i: (layer, 0, 0)),
            row_spec,
            pl.BlockSpec((None, 1, n), lambda i: (gain_index, 0, 0)),
        ],
        out_specs=[row_spec] * len(out_shapes),
        out_shape=out_shapes,
        compiler_params=_params("parallel"),
        name="out_proj_residual",
    )(a_bf16, w_bf16, res, gains.reshape(-1, 1, n))


def _bias_tiles_kernel(rb_ref, o_ref, *, t):
    h = pl.program_id(0)
    blk = REL_MAX_DIST
    nb = t // blk
    qi = lax.broadcasted_iota(jnp.int32, (blk, blk), 0)
    ki = lax.broadcasted_iota(jnp.int32, (blk, blk), 1)
    max_exact = REL_BUCKETS // 2
    far = rb_ref[REL_BUCKETS - 1, h]

    def band(offset):
        n = jnp.maximum(qi - ki + offset, 0)
        nf = jnp.maximum(n, 1).astype(F32)
        large = max_exact + (jnp.log(nf / max_exact) / math.log(REL_MAX_DIST / max_exact)
                             * (REL_BUCKETS - max_exact)).astype(jnp.int32)
        large = jnp.minimum(large, REL_BUCKETS - 1)
        bucket = jnp.where(n < max_exact, n, large)
        val = jnp.zeros((blk, blk), F32)
        for b in range(REL_BUCKETS - 1):
            val = jnp.where(bucket == b, rb_ref[b, h] - far, val)
        return val

    on_diag = jnp.where(qi >= ki, band(0), NEG_BIG)
    below = band(blk)
    zeros = jnp.zeros((blk, blk), F32)
    masked = jnp.full((blk, blk), NEG_BIG, F32)
    for bi in range(nb):
        for bj in range(nb):
            rows, cols = slice(bi * blk, (bi + 1) * blk), slice(bj * blk, (bj + 1) * blk)
            o_ref[0, 0, rows, cols] = (on_diag if bj == bi else below if bj == bi - 1
                                       else masked if bj > bi else zeros)
            o_ref[0, 1, rows, cols] = below if (bi == 0 and bj == nb - 1) else zeros


def bias_tiles(rel_bias, t):
    assert t % REL_MAX_DIST == 0
    heads = rel_bias.shape[1]
    return pl.pallas_call(
        functools.partial(_bias_tiles_kernel, t=t),
        grid=(heads,),
        in_specs=[pl.BlockSpec(memory_space=pltpu.SMEM)],
        out_specs=pl.BlockSpec((1, 2, t, t), lambda h: (h, 0, 0, 0)),
        out_shape=jax.ShapeDtypeStruct((heads, 2, t, t), F32),
        compiler_params=_params("parallel"),
        name="t5_bias_tiles",
    )(rel_bias)


def _da_kernel(lamv_ref, q_ref, k_ref, v_ref, z_ref, bias_ref, g_ref, o_ref,
               kt_scr, v_scr, q_scr, m_scr, acc_scr, *, t, hp, lam_init):
    qi = pl.program_id(2)
    nt = kt_scr.shape[1]
    hd = DA_V_DIM

    @pl.when(qi == 0)
    def _():
        for h in range(hp):
            for jj in range(nt):
                kt_scr[h, jj] = k_ref[0, jj * t:(jj + 1) * t, h * hd:(h + 1) * hd].astype(F32).T.astype(BF16)
            v_scr[h, :, 0:hd] = v_ref[0, :, h * hd:(h + 1) * hd]
            v_scr[h, :, hd:2 * hd] = jnp.ones((v_scr.shape[1], hd), BF16)

    lane = lax.broadcasted_iota(jnp.int32, (t, hd), 1)
    for h in range(hp):
        q = q_ref[0, :, h * hd:(h + 1) * hd] * (DA_QK_DIM ** -0.5)
        q_scr[h, 0] = jnp.where(lane < DA_QK_DIM, q, 0.0).astype(BF16)
        q_scr[h, 1] = jnp.where(lane >= DA_QK_DIM, q, 0.0).astype(BF16)
    m_scr[...] = jnp.full(m_scr.shape, -jnp.inf, F32)
    acc_scr[...] = jnp.zeros(acc_scr.shape, F32)

    def tile(j, bias_sel):
        off = pl.multiple_of(j * t, t)
        for h in range(hp):
            kt = kt_scr[h, j]
            v = v_scr[h, pl.ds(off, t), :]
            for c in range(2):
                s = _dot(q_scr[h, c], kt)
                if bias_sel is not None:
                    s = s + bias_ref[h, bias_sel]
                m_prev = m_scr[h, c]
                m_new = jnp.maximum(m_prev, jnp.max(s, axis=-1, keepdims=True))
                alpha = jnp.exp(m_prev - m_new)
                p = jnp.exp(s - _lane_tile(m_new, t // LANES))
                acc_scr[h, c] = _lane_tile(alpha, 2) * acc_scr[h, c] + _dot(p.astype(BF16), v)
                m_scr[h, c] = m_new

    n_far = jnp.maximum(qi - 1, 0)

    def far_body(jp, carry):
        tile(2 * jp, None)
        tile(2 * jp + 1, None)
        return carry

    lax.fori_loop(0, n_far // 2, far_body, 0)

    @pl.when(n_far % 2 == 1)
    def _():
        tile(n_far - 1, None)
        tile(qi - 1, 1)
        tile(qi, 0)

    @pl.when((qi >= 1) & (n_far % 2 == 0))
    def _():
        tile(qi - 1, 1)
        tile(qi, 0)

    @pl.when(qi == 0)
    def _():
        tile(0, 0)

    lv = lamv_ref[...]
    s1 = jnp.sum(lv[0:1] * lv[1:2], axis=-1, keepdims=True)
    s2 = jnp.sum(lv[2:3] * lv[3:4], axis=-1, keepdims=True)
    lam = jnp.exp(s1) - jnp.exp(s2) + lam_init
    for h in range(hp):
        a0 = acc_scr[h, 0]
        a1 = acc_scr[h, 1]
        o = a0[:, 0:hd] / a0[:, hd:2 * hd] - lam * (a1[:, 0:hd] / a1[:, hd:2 * hd])
        y = o * lax.rsqrt(jnp.mean(o * o, axis=-1, keepdims=True) + EPS)
        y = (y * g_ref[:, h * hd:(h + 1) * hd]) * (1.0 - lam_init)
        o_ref[0, :, h * hd:(h + 1) * hd] = (y * _silu(z_ref[0, :, h * hd:(h + 1) * hd])).astype(o_ref.dtype)


def diff_attention_branch(qkv, rest, bias, lamv, da_norm_g, layer, lam_init, t, hp):
    b, s, _ = qkv.shape
    t = _tile(s, t)
    hd = DA_V_DIM
    w = hp * hd
    cpb = BRANCH_WIDTH // w
    return pl.pallas_call(
        functools.partial(_da_kernel, t=t, hp=hp, lam_init=lam_init),
        grid=(b, DA_HEADS // hp, s // t),
        in_specs=[
            pl.BlockSpec((None, 4, DA_QK_DIM), lambda bi, h, qi: (layer, 0, 0)),
            pl.BlockSpec((1, t, w), lambda bi, h, qi: (bi, qi, QKV_DA_Q * cpb + h)),
            pl.BlockSpec((1, s, w), lambda bi, h, qi: (bi, 0, QKV_DA_K * cpb + h)),
            pl.BlockSpec((1, s, w), lambda bi, h, qi: (bi, 0, QKV_DA_V * cpb + h)),
            pl.BlockSpec((1, t, w), lambda bi, h, qi: (bi, qi, REST_DA_Z * cpb + h)),
            pl.BlockSpec((hp, 2, t, t), lambda bi, h, qi: (h, 0, 0, 0)),
            pl.BlockSpec((None, 1, w), lambda bi, h, qi: (layer, 0, h)),
        ],
        out_specs=pl.BlockSpec((1, t, w), lambda bi, h, qi: (bi, qi, h)),
        out_shape=jax.ShapeDtypeStruct((b, s, BRANCH_WIDTH), BF16),
        scratch_shapes=[
            pltpu.VMEM((hp, s // t, hd, t), BF16),
            pltpu.VMEM((hp, s, 2 * hd), BF16),
            pltpu.VMEM((hp, 2, t, hd), BF16),
            pltpu.VMEM((hp, 2, t, LANES), F32),
            pltpu.VMEM((hp, 2, t, 2 * hd), F32),
        ],
        compiler_params=_params("parallel", "parallel", "arbitrary"),
        name="diff_attention",
    )(lamv, qkv, qkv, qkv, rest, bias, da_norm_g.reshape(-1, 1, BRANCH_WIDTH))


def _sb_kernel(q_ref, k_ref, v_ref, z_ref, o_ref, kt_scr, q_scr, c_scr, acc_scr, *, t, hp):
    qi = pl.program_id(2)
    nt = kt_scr.shape[1]
    hd = SB_HEAD_DIM
    cb = min(SB_CUMSUM_BLOCK, t)

    @pl.when(qi == 0)
    def _():
        for h in range(hp):
            for jj in range(nt):
                kt_scr[h, jj] = k_ref[0, jj * t:(jj + 1) * t, h * hd:(h + 1) * hd].astype(F32).T.astype(BF16)

    for h in range(hp):
        q_scr[h] = (q_ref[0, :, h * hd:(h + 1) * hd].astype(F32) * (SB_HEAD_DIM ** -0.5 * LOG2E)).astype(BF16)
    c_scr[...] = jnp.zeros(c_scr.shape, F32)
    acc_scr[...] = jnp.zeros(acc_scr.shape, F32)

    def tile(j, diag):
        off = pl.multiple_of(j * t, t)
        r2 = lax.broadcasted_iota(jnp.int32, (cb, cb), 0)
        c2 = lax.broadcasted_iota(jnp.int32, (cb, cb), 1)
        upper = jnp.where(r2 >= c2, 1.0, 0.0).astype(BF16)
        if diag:
            row = lax.broadcasted_iota(jnp.int32, (t, t), 0)
            col = lax.broadcasted_iota(jnp.int32, (t, t), 1)
            causal = col < row
        for h in range(hp):
            v = v_ref[0, pl.ds(off, t), h * hd:(h + 1) * hd]
            z2 = _dot(q_scr[h], kt_scr[h, j])
            w = jnp.maximum(z2, 0.0) + jnp.log(1.0 + jnp.exp2(_neg_abs(z2))) * (1.0 / LN2)
            if diag:
                w = jnp.where(causal, w, 0.0)
            c = c_scr[h]
            parts = []
            for blk in reversed(range(t // cb)):
                sl = slice(blk * cb, (blk + 1) * cb)
                cs = _dot(w[:, sl].astype(BF16), upper)
                parts.append((z2[:, sl] - _lane_tile(c, cb // LANES)) - cs)
                c = c + jnp.broadcast_to(cs[:, 0:1], c.shape)
            c_scr[h] = c
            a = jnp.exp2(jnp.concatenate(parts[::-1], axis=1))
            if diag:
                a = jnp.where(causal, a, 0.0)
            acc_scr[h] += _dot(a.astype(BF16), v)

    n_rest = jnp.maximum(qi - 1, 0)
    odd = n_rest % 2

    @pl.when(qi == 0)
    def _():
        tile(0, True)

    @pl.when((qi >= 1) & (odd == 0))
    def _():
        tile(qi, True)
        tile(qi - 1, False)

    @pl.when(odd == 1)
    def _():
        tile(qi, True)
        tile(qi - 1, False)
        tile(qi - 2, False)

    def body(jp, carry):
        tile(qi - 2 - odd - 2 * jp, False)
        tile(qi - 3 - odd - 2 * jp, False)
        return carry

    lax.fori_loop(0, n_rest // 2, body, 0)

    for h in range(hp):
        o_ref[0, :, h * hd:(h + 1) * hd] = (acc_scr[h] * _silu(z_ref[0, :, h * hd:(h + 1) * hd])).astype(o_ref.dtype)


def stick_breaking_branch(qkv, rest, t, hp):
    b, s, _ = qkv.shape
    t = _tile(s, t)
    hd = SB_HEAD_DIM
    w = hp * hd
    cpb = BRANCH_WIDTH // w
    return pl.pallas_call(
        functools.partial(_sb_kernel, t=t, hp=hp),
        grid=(b, SB_HEADS // hp, s // t),
        in_specs=[
            pl.BlockSpec((1, t, w), lambda bi, h, qi: (bi, qi, QKV_SB_Q * cpb + h)),
            pl.BlockSpec((1, s, w), lambda bi, h, qi: (bi, 0, QKV_SB_K * cpb + h)),
            pl.BlockSpec((1, s, w), lambda bi, h, qi: (bi, 0, QKV_SB_V * cpb + h)),
            pl.BlockSpec((1, t, w), lambda bi, h, qi: (bi, qi, REST_SB_Z * cpb + h)),
        ],
        out_specs=pl.BlockSpec((1, t, w), lambda bi, h, qi: (bi, qi, h)),
        out_shape=jax.ShapeDtypeStruct((b, s, BRANCH_WIDTH), BF16),
        scratch_shapes=[
            pltpu.VMEM((hp, s // t, hd, t), BF16),
            pltpu.VMEM((hp, t, hd), BF16),
            pltpu.VMEM((hp, t, LANES), F32),
            pltpu.VMEM((hp, t, hd), F32),
        ],
        compiler_params=_params("parallel", "parallel", "arbitrary"),
        name="stick_breaking",
    )(qkv, qkv, qkv, rest)


def _pool_kernel(u_ref, z_ref, w_ref, sc_ref, o_ref, ext_scr, *, tt):
    ti = pl.program_id(1)

    @pl.when(ti == 0)
    def _():
        ext_scr[0:POOL_HALO] = jnp.zeros((POOL_HALO, BRANCH_WIDTH), F32)

    ext_scr[POOL_HALO:POOL_HALO + tt] = u_ref[0]
    pos = ti * tt + lax.broadcasted_iota(jnp.int32, (tt, 1), 0)
    for g, w in enumerate(POOL_WINDOWS):
        cols = slice(g * POOL_GROUP, (g + 1) * POOL_GROUP)
        u = ext_scr[POOL_HALO:POOL_HALO + tt, cols]
        win = u
        for j in range(1, w):
            win = win + ext_scr[POOL_HALO - j:POOL_HALO - j + tt, cols]
        count = jnp.minimum(pos + 1, w).astype(F32)
        pooled = win / count - u
        mixed = _dot(pooled.astype(BF16), w_ref[g]) * sc_ref[:, cols]
        o_ref[0, :, cols] = (mixed * _silu(z_ref[0, :, cols])).astype(o_ref.dtype)
    ext_scr[0:POOL_HALO] = ext_scr[tt:tt + POOL_HALO]


def pool_branch(rest, w_pool_bf16, pool_scale, layer, tt=1024):
    b, s, _ = rest.shape
    tt = _tile(s, tt)
    assert tt >= POOL_HALO and max(POOL_WINDOWS) <= POOL_HALO
    return pl.pallas_call(
        functools.partial(_pool_kernel, tt=tt),
        grid=(b, s // tt),
        in_specs=[
            pl.BlockSpec((1, tt, BRANCH_WIDTH), lambda bi, ti: (bi, ti, REST_POOL_U)),
            pl.BlockSpec((1, tt, BRANCH_WIDTH), lambda bi, ti: (bi, ti, REST_POOL_Z)),
            pl.BlockSpec((None,) + w_pool_bf16.shape[1:], lambda bi, ti: (layer, 0, 0, 0)),
            pl.BlockSpec((None, 1, BRANCH_WIDTH), lambda bi, ti: (layer, 0, 0)),
        ],
        out_specs=pl.BlockSpec((1, tt, BRANCH_WIDTH), lambda bi, ti: (bi, ti, 0)),
        out_shape=jax.ShapeDtypeStruct((b, s, BRANCH_WIDTH), BF16),
        scratch_shapes=[pltpu.VMEM((tt + POOL_HALO, BRANCH_WIDTH), F32)],
        compiler_params=_params("parallel", "arbitrary"),
        name="multiscale_pool",
    )(rest, rest, w_pool_bf16, pool_scale.reshape(-1, 1, BRANCH_WIDTH))


def _mem_attn_kernel(q_ref, z_ref, mk_ref, mv_ref, o_ref):
    hd = MEM_HEAD_DIM
    for h in range(MEM_HEADS):
        cols = slice(h * hd, (h + 1) * hd)
        q = q_ref[0, :, cols] * (MEM_HEAD_DIM ** -0.5)
        s = _dot_nt(q, mk_ref[0, :, cols].astype(BF16))
        p = jnp.exp(s - jnp.max(s, axis=-1, keepdims=True))
        l = jnp.sum(p, axis=-1, keepdims=True)
        o = _dot(p.astype(BF16), mv_ref[0, :, cols].astype(BF16)) / l
        o_ref[0, :, cols] = (o * _silu(z_ref[0, :, cols])).astype(o_ref.dtype)


def memory_branch(qkv, rest, mkv, tq=1024):
    b, s, _ = qkv.shape
    m = mkv.shape[1]
    tq = _tile(s, tq)
    w = BRANCH_WIDTH
    return pl.pallas_call(
        _mem_attn_kernel,
        grid=(b, s // tq),
        in_specs=[
            pl.BlockSpec((1, tq, w), lambda bi, qi: (bi, qi, QKV_MEM_Q)),
            pl.BlockSpec((1, tq, w), lambda bi, qi: (bi, qi, REST_MEM_Z)),
            pl.BlockSpec((1, m, w), lambda bi, qi: (bi, 0, 0)),
            pl.BlockSpec((1, m, w), lambda bi, qi: (bi, 0, 1)),
        ],
        out_specs=pl.BlockSpec((1, tq, w), lambda bi, qi: (bi, qi, 0)),
        out_shape=jax.ShapeDtypeStruct((b, s, BRANCH_WIDTH), BF16),
        compiler_params=_params("parallel", "parallel"),
        name="memory_attention",
    )(qkv, rest, mkv, mkv)


def _merge_kernel(b0_ref, b1_ref, b2_ref, b3_ref, g0_ref, g1_ref, g2_ref, g3_ref, gb_ref, w_ref, o_ref):
    branches = (b0_ref, b1_ref, b2_ref, b3_ref)
    gates = (g0_ref, g1_ref, g2_ref, g3_ref)
    merged = None
    for n in range(N_BRANCH):
        gate = 1.0 / (1.0 + jnp.exp(-(gates[n][...] + gb_ref[n:n + 1, :])))
        term = gate * _dot(branches[n][...], w_ref[n])
        merged = term if merged is None else merged + term
    o_ref[...] = merged.astype(o_ref.dtype)


def gated_merge(branches, rest2d, gate_b, w_branch_bf16, layer, tm=512, tn=1024):
    m = rest2d.shape[0]
    d = w_branch_bf16.shape[3]
    tm, tn = _tile(m, tm), _tile(d, tn)
    gate_col0 = REST_GATES * BRANCH_WIDTH // tn
    per_gate = d // tn
    br_spec = pl.BlockSpec((tm, BRANCH_WIDTH), lambda j, i: (i, 0))

    def gate_spec(n):
        return pl.BlockSpec((tm, tn), lambda j, i: (i, gate_col0 + n * per_gate + j))

    return pl.pallas_call(
        _merge_kernel,
        grid=(d // tn, m // tm),
        in_specs=[br_spec] * N_BRANCH + [gate_spec(n) for n in range(N_BRANCH)] + [
            pl.BlockSpec((None, N_BRANCH, tn), lambda j, i: (layer, 0, j)),
            pl.BlockSpec((None, N_BRANCH, BRANCH_WIDTH, tn), lambda j, i: (layer, 0, 0, j)),
        ],
        out_specs=pl.BlockSpec((tm, tn), lambda j, i: (i, j)),
        out_shape=jax.ShapeDtypeStruct((m, d), BF16),
        compiler_params=_params("parallel", "parallel"),
        name="gated_merge",
    )(*branches, rest2d, rest2d, rest2d, rest2d, gate_b, w_branch_bf16)


def kernel(x, mem, rel_bias, norm_g, w_in, gate_b, lam_q1, lam_k1, lam_q2, lam_k2, da_norm_g, w_pool,
           pool_scale, mem_norm_g, w_mem_kv, w_branch, w_out, final_g):
    b, s, d = x.shape
    depth = norm_g.shape[0]
    n_mem = mem.shape[1]
    t = min(ATTN_TILE, s)
    hp = ATTN_HEADS_PER_STEP

    w_in_bf, w_pool_bf, w_mem_kv_bf = w_in.astype(BF16), w_pool.astype(BF16), w_mem_kv.astype(BF16)
    w_branch_bf, w_out_bf = w_branch.astype(BF16), w_out.astype(BF16)
    lamv = jnp.stack([lam_q1, lam_k1, lam_q2, lam_k2], axis=1)
    n_gate_blocks = (w_in.shape[2] - N_SLICES * BRANCH_WIDTH) // BRANCH_WIDTH
    rest_blocks = REST_SLICES + tuple(range(N_SLICES, N_SLICES + n_gate_blocks))
    mkv_blocks = tuple(range(w_mem_kv.shape[2] // BRANCH_WIDTH))

    bias = bias_tiles(rel_bias, t)
    x2d = x.reshape(b * s, d)
    mem2d = mem.reshape(b * n_mem, d)
    h = rmsnorm_bf16(x2d, norm_g, 0)
    for l in range(depth):
        lam_init = 0.8 - 0.6 * math.exp(-0.3 * l)
        qkv = matmul_cols(h, w_in_bf, l, QKV_SLICES, BF16).reshape(b, s, -1)
        rest2d = matmul_cols(h, w_in_bf, l, rest_blocks, F32)
        rest = rest2d.reshape(b, s, -1)

        br_da = diff_attention_branch(qkv, rest, bias, lamv, da_norm_g, l, lam_init, t, hp)
        br_sb = stick_breaking_branch(qkv, rest, t, hp)
        br_pool = pool_branch(rest, w_pool_bf, pool_scale, l)
        mkv = matmul_cols(rmsnorm_bf16(mem2d, mem_norm_g, l), w_mem_kv_bf, l, mkv_blocks, F32)
        br_mem = memory_branch(qkv, rest, mkv.reshape(b, n_mem, 2 * BRANCH_WIDTH))

        branches = [br.reshape(b * s, BRANCH_WIDTH) for br in (br_da, br_sb, br_pool, br_mem)]
        merged = gated_merge(branches, rest2d, gate_b, w_branch_bf, l)
        if l == depth - 1:
            (out2d,) = matmul_residual(merged, w_out_bf, l, x2d, final_g, 0, final_norm=True)
        else:
            x2d, h = matmul_residual(merged, w_out_bf, l, x2d, norm_g, l + 1, final_norm=False)
    return out2d.reshape(b, s, d)
```

```python
import functools
import math

import jax
import jax.numpy as jnp
from jax import lax
from jax.experimental import pallas as pl
from jax.experimental.pallas import tpu as pltpu

F32 = jnp.float32
BF16 = jnp.bfloat16

BRANCH_WIDTH = 1024
N_BRANCH = 4
N_SLICES = 12
DA_HEADS = 8
DA_QK_DIM = 64
DA_V_DIM = 2 * DA_QK_DIM
SB_HEADS = 8
SB_HEAD_DIM = BRANCH_WIDTH // SB_HEADS
POOL_WINDOWS = (2, 4, 8, 16)
POOL_GROUP = BRANCH_WIDTH // len(POOL_WINDOWS)
MEM_HEADS = 4
MEM_HEAD_DIM = BRANCH_WIDTH // MEM_HEADS
REL_BUCKETS = 32
REL_MAX_DIST = 128
EPS = 1e-6

SL_DA_Q, SL_DA_K, SL_DA_V, SL_DA_Z = 0, 1, 2, 3
SL_SB_Q, SL_SB_K, SL_SB_V, SL_SB_Z = 4, 5, 6, 7
SL_POOL_U, SL_POOL_Z, SL_MEM_Q, SL_MEM_Z = 8, 9, 10, 11
QKV_SLICES = (SL_DA_Q, SL_DA_K, SL_DA_V, SL_SB_Q, SL_SB_K, SL_SB_V, SL_MEM_Q)
REST_SLICES = (SL_DA_Z, SL_SB_Z, SL_POOL_U, SL_POOL_Z, SL_MEM_Z)
QKV_DA_Q, QKV_DA_K, QKV_DA_V, QKV_SB_Q, QKV_SB_K, QKV_SB_V, QKV_MEM_Q = range(len(QKV_SLICES))
REST_DA_Z, REST_SB_Z, REST_POOL_U, REST_POOL_Z, REST_MEM_Z, REST_GATES = range(len(REST_SLICES) + 1)

V7X_VMEM_LIMIT_BYTES = 56 * 1024 * 1024
LANES = 128
POOL_HALO = 16
NEG_BIG = -1e30
ATTN_TILE = 512
ATTN_HEADS_PER_STEP = 2
SB_CUMSUM_BLOCK = 256
LOG2E = 1.4426950408889634
LN2 = 0.6931471805599453


def _params(*sem):
    return pltpu.CompilerParams(dimension_semantics=sem, vmem_limit_bytes=V7X_VMEM_LIMIT_BYTES)


def _tile(n, t):
    t = min(t, n)
    assert n % t == 0, (n, t)
    return t


def _silu(z):
    return z * (1.0 / (1.0 + jnp.exp(-z)))


def _dot_nt(a, b):
    return lax.dot_general(a, b, (((1,), (1,)), ((), ())), preferred_element_type=F32)


def _dot(a, b):
    return jnp.dot(a, b, preferred_element_type=F32)


def _lane_tile(x, n):
    return x if n == 1 else jnp.concatenate([x] * n, axis=1)


def _neg_abs(x):
    u = lax.bitcast_convert_type(x, jnp.uint32) | jnp.uint32(0x80000000)
    return lax.bitcast_convert_type(u, F32)


def _rmsnorm_kernel(x_ref, g_ref, o_ref):
    x = x_ref[...]
    y = x * lax.rsqrt(jnp.mean(x * x, axis=-1, keepdims=True) + EPS)
    o_ref[...] = (y * g_ref[...]).astype(o_ref.dtype)


def rmsnorm_bf16(x2d, gains, layer, tr=512):
    m, d = x2d.shape
    tr = _tile(m, tr)
    return pl.pallas_call(
        _rmsnorm_kernel,
        grid=(m // tr,),
        in_specs=[
            pl.BlockSpec((tr, d), lambda i: (i, 0)),
            pl.BlockSpec((None, 1, d), lambda i: (layer, 0, 0)),
        ],
        out_specs=pl.BlockSpec((tr, d), lambda i: (i, 0)),
        out_shape=jax.ShapeDtypeStruct((m, d), BF16),
        compiler_params=_params("parallel"),
        name="rmsnorm",
    )(x2d, gains.reshape(-1, 1, d))


def _matmul_kernel(h_ref, w_ref, o_ref):
    o_ref[...] = _dot(h_ref[...], w_ref[...]).astype(o_ref.dtype)


def _static_lookup(table, j):
    r = jnp.int32(table[0])
    for idx in range(1, len(table)):
        r = jnp.where(j == idx, jnp.int32(table[idx]), r)
    return r


def matmul_cols(h, w, layer, col_blocks, out_dtype, tm=2048, tn=BRANCH_WIDTH):
    m, d = h.shape
    tm = _tile(m, tm)
    assert w.shape[2] % tn == 0
    n_out = len(col_blocks)
    return pl.pallas_call(
        _matmul_kernel,
        grid=(m // tm, n_out),
        in_specs=[
            pl.BlockSpec((tm, d), lambda i, j: (i, 0)),
            pl.BlockSpec((None, d, tn), lambda i, j: (layer, 0, _static_lookup(col_blocks, j))),
        ],
        out_specs=pl.BlockSpec((tm, tn), lambda i, j: (i, j)),
        out_shape=jax.ShapeDtypeStruct((m, n_out * tn), out_dtype),
        compiler_params=_params("parallel", "parallel"),
        name="in_proj",
    )(h, w)


def _matmul_residual_kernel(a_ref, w_ref, r_ref, g_ref, *out_refs, final_norm):
    y = r_ref[...] + _dot(a_ref[...], w_ref[...])
    normed = (y * lax.rsqrt(jnp.mean(y * y, axis=-1, keepdims=True) + EPS)) * g_ref[...]
    if final_norm:
        out_refs[0][...] = normed
    else:
        out_refs[0][...] = y
        out_refs[1][...] = normed.astype(BF16)


def matmul_residual(a_bf16, w_bf16, layer, res, gains, gain_index, final_norm, tm=512):
    m, k = a_bf16.shape
    n = w_bf16.shape[2]
    tm = _tile(m, tm)
    row_spec = pl.BlockSpec((tm, n), lambda i: (i, 0))
    out_shapes = [jax.ShapeDtypeStruct((m, n), F32)]
    if not final_norm:
        out_shapes.append(jax.ShapeDtypeStruct((m, n), BF16))
    return pl.pallas_call(
        functools.partial(_matmul_residual_kernel, final_norm=final_norm),
        grid=(m // tm,),
        in_specs=[
            pl.BlockSpec((tm, k), lambda i: (i, 0)),
            pl.BlockSpec((None, k, n), lambda i: (layer, 0, 0)),
            row_spec,
            pl.BlockSpec((None, 1, n), lambda i: (gain_index, 0, 0)),
        ],
        out_specs=[row_spec] * len(out_shapes),
        out_shape=out_shapes,
        compiler_params=_params("parallel"),
        name="out_proj_residual",
    )(a_bf16, w_bf16, res, gains.reshape(-1, 1, n))


def _bias_tiles_kernel(rb_ref, o_ref, *, t):
    h = pl.program_id(0)
    blk = REL_MAX_DIST
    nb = t // blk
    qi = lax.broadcasted_iota(jnp.int32, (blk, blk), 0)
    ki = lax.broadcasted_iota(jnp.int32, (blk, blk), 1)
    max_exact = REL_BUCKETS // 2
    far = rb_ref[REL_BUCKETS - 1, h]

    def band(offset):
        n = jnp.maximum(qi - ki + offset, 0)
        nf = jnp.maximum(n, 1).astype(F32)
        large = max_exact + (jnp.log(nf / max_exact) / math.log(REL_MAX_DIST / max_exact)
                             * (REL_BUCKETS - max_exact)).astype(jnp.int32)
        large = jnp.minimum(large, REL_BUCKETS - 1)
        bucket = jnp.where(n < max_exact, n, large)
        val = jnp.zeros((blk, blk), F32)
        for b in range(REL_BUCKETS - 1):
            val = jnp.where(bucket == b, rb_ref[b, h] - far, val)
        return val

    on_diag = jnp.where(qi >= ki, band(0), NEG_BIG)
    below = band(blk)
    zeros = jnp.zeros((blk, blk), F32)
    masked = jnp.full((blk, blk), NEG_BIG, F32)
    for bi in range(nb):
        for bj in range(nb):
            rows, cols = slice(bi * blk, (bi + 1) * blk), slice(bj * blk, (bj + 1) * blk)
            o_ref[0, 0, rows, cols] = (on_diag if bj == bi else below if bj == bi - 1
                                       else masked if bj > bi else zeros)
            o_ref[0, 1, rows, cols] = below if (bi == 0 and bj == nb - 1) else zeros


def bias_tiles(rel_bias, t):
    assert t % REL_MAX_DIST == 0
    heads = rel_bias.shape[1]
    return pl.pallas_call(
        functools.partial(_bias_tiles_kernel, t=t),
        grid=(heads,),
        in_specs=[pl.BlockSpec(memory_space=pltpu.SMEM)],
        out_specs=pl.BlockSpec((1, 2, t, t), lambda h: (h, 0, 0, 0)),
        out_shape=jax.ShapeDtypeStruct((heads, 2, t, t), F32),
        compiler_params=_params("parallel"),
        name="t5_bias_tiles",
    )(rel_bias)


def _da_kernel(lamv_ref, q_ref, k_ref, v_ref, z_ref, bias_ref, g_ref, o_ref,
               kt_scr, v_scr, q_scr, m_scr, acc_scr, *, t, hp, lam_init):
    qi = pl.program_id(2)
    nt = kt_scr.shape[1]
    hd = DA_V_DIM

    @pl.when(qi == 0)
    def _():
        for h in range(hp):
            for jj in range(nt):
                kt_scr[h, jj] = k_ref[0, jj * t:(jj + 1) * t, h * hd:(h + 1) * hd].astype(F32).T.astype(BF16)
            v_scr[h, :, 0:hd] = v_ref[0, :, h * hd:(h + 1) * hd]
            v_scr[h, :, hd:2 * hd] = jnp.ones((v_scr.shape[1], hd), BF16)

    lane = lax.broadcasted_iota(jnp.int32, (t, hd), 1)
    for h in range(hp):
        q = q_ref[0, :, h * hd:(h + 1) * hd] * (DA_QK_DIM ** -0.5)
        q_scr[h, 0] = jnp.where(lane < DA_QK_DIM, q, 0.0).astype(BF16)
        q_scr[h, 1] = jnp.where(lane >= DA_QK_DIM, q, 0.0).astype(BF16)
    m_scr[...] = jnp.full(m_scr.shape, -jnp.inf, F32)
    acc_scr[...] = jnp.zeros(acc_scr.shape, F32)

    def tile(j, bias_sel):
        off = pl.multiple_of(j * t, t)
        for h in range(hp):
            kt = kt_scr[h, j]
            v = v_scr[h, pl.ds(off, t), :]
            for c in range(2):
                s = _dot(q_scr[h, c], kt)
                if bias_sel is not None:
                    s = s + bias_ref[h, bias_sel]
                m_prev = m_scr[h, c]
                m_new = jnp.maximum(m_prev, jnp.max(s, axis=-1, keepdims=True))
                alpha = jnp.exp(m_prev - m_new)
                p = jnp.exp(s - _lane_tile(m_new, t // LANES))
                acc_scr[h, c] = _lane_tile(alpha, 2) * acc_scr[h, c] + _dot(p.astype(BF16), v)
                m_scr[h, c] = m_new

    n_far = jnp.maximum(qi - 1, 0)

    def far_body(jp, carry):
        tile(2 * jp, None)
        tile(2 * jp + 1, None)
        return carry

    lax.fori_loop(0, n_far // 2, far_body, 0)

    @pl.when(n_far % 2 == 1)
    def _():
        tile(n_far - 1, None)
        tile(qi - 1, 1)
        tile(qi, 0)

    @pl.when((qi >= 1) & (n_far % 2 == 0))
    def _():
        tile(qi - 1, 1)
        tile(qi, 0)

    @pl.when(qi == 0)
    def _():
        tile(0, 0)

    lv = lamv_ref[...]
    s1 = jnp.sum(lv[0:1] * lv[1:2], axis=-1, keepdims=True)
    s2 = jnp.sum(lv[2:3] * lv[3:4], axis=-1, keepdims=True)
    lam = jnp.exp(s1) - jnp.exp(s2) + lam_init
    for h in range(hp):
        a0 = acc_scr[h, 0]
        a1 = acc_scr[h, 1]
        o = a0[:, 0:hd] / a0[:, hd:2 * hd] - lam * (a1[:, 0:hd] / a1[:, hd:2 * hd])
        y = o * lax.rsqrt(jnp.mean(o * o, axis=-1, keepdims=True) + EPS)
        y = (y * g_ref[:, h * hd:(h + 1) * hd]) * (1.0 - lam_init)
        o_ref[0, :, h * hd:(h + 1) * hd] = (y * _silu(z_ref[0, :, h * hd:(h + 1) * hd])).astype(o_ref.dtype)


def diff_attention_branch(qkv, rest, bias, lamv, da_norm_g, layer, lam_init, t, hp):
    b, s, _ = qkv.shape
    t = _tile(s, t)
    hd = DA_V_DIM
    w = hp * hd
    cpb = BRANCH_WIDTH // w
    return pl.pallas_call(
        functools.partial(_da_kernel, t=t, hp=hp, lam_init=lam_init),
        grid=(b, DA_HEADS // hp, s // t),
        in_specs=[
            pl.BlockSpec((None, 4, DA_QK_DIM), lambda bi, h, qi: (layer, 0, 0)),
            pl.BlockSpec((1, t, w), lambda bi, h, qi: (bi, qi, QKV_DA_Q * cpb + h)),
            pl.BlockSpec((1, s, w), lambda bi, h, qi: (bi, 0, QKV_DA_K * cpb + h)),
            pl.BlockSpec((1, s, w), lambda bi, h, qi: (bi, 0, QKV_DA_V * cpb + h)),
            pl.BlockSpec((1, t, w), lambda bi, h, qi: (bi, qi, REST_DA_Z * cpb + h)),
            pl.BlockSpec((hp, 2, t, t), lambda bi, h, qi: (h, 0, 0, 0)),
            pl.BlockSpec((None, 1, w), lambda bi, h, qi: (layer, 0, h)),
        ],
        out_specs=pl.BlockSpec((1, t, w), lambda bi, h, qi: (bi, qi, h)),
        out_shape=jax.ShapeDtypeStruct((b, s, BRANCH_WIDTH), BF16),
        scratch_shapes=[
            pltpu.VMEM((hp, s // t, hd, t), BF16),
            pltpu.VMEM((hp, s, 2 * hd), BF16),
            pltpu.VMEM((hp, 2, t, hd), BF16),
            pltpu.VMEM((hp, 2, t, LANES), F32),
            pltpu.VMEM((hp, 2, t, 2 * hd), F32),
        ],
        compiler_params=_params("parallel", "parallel", "arbitrary"),
        name="diff_attention",
    )(lamv, qkv, qkv, qkv, rest, bias, da_norm_g.reshape(-1, 1, BRANCH_WIDTH))


def _sb_kernel(q_ref, k_ref, v_ref, z_ref, o_ref, kt_scr, q_scr, c_scr, acc_scr, *, t, hp):
    qi = pl.program_id(2)
    nt = kt_scr.shape[1]
    hd = SB_HEAD_DIM
    cb = min(SB_CUMSUM_BLOCK, t)

    @pl.when(qi == 0)
    def _():
        for h in range(hp):
            for jj in range(nt):
                kt_scr[h, jj] = k_ref[0, jj * t:(jj + 1) * t, h * hd:(h + 1) * hd].astype(F32).T.astype(BF16)

    for h in range(hp):
        q_scr[h] = (q_ref[0, :, h * hd:(h + 1) * hd].astype(F32) * (SB_HEAD_DIM ** -0.5 * LOG2E)).astype(BF16)
    c_scr[...] = jnp.zeros(c_scr.shape, F32)
    acc_scr[...] = jnp.zeros(acc_scr.shape, F32)

    def tile(j, diag):
        off = pl.multiple_of(j * t, t)
        r2 = lax.broadcasted_iota(jnp.int32, (cb, cb), 0)
        c2 = lax.broadcasted_iota(jnp.int32, (cb, cb), 1)
        upper = jnp.where(r2 > c2, 1.0, 0.0).astype(BF16)
        if diag:
            row = lax.broadcasted_iota(jnp.int32, (t, t), 0)
            col = lax.broadcasted_iota(jnp.int32, (t, t), 1)
            causal = col < row
        for h in range(hp):
            v = v_ref[0, pl.ds(off, t), h * hd:(h + 1) * hd]
            z2 = _dot(q_scr[h], kt_scr[h, j])
            w = jnp.maximum(z2, 0.0) + jnp.log(1.0 + jnp.exp2(_neg_abs(z2))) * (1.0 / LN2)
            log2_beta = z2 - w
            if diag:
                w = jnp.where(causal, w, 0.0)
            c = c_scr[h]
            parts = []
            for blk in reversed(range(t // cb)):
                sl = slice(blk * cb, (blk + 1) * cb)
                wblk = w[:, sl]
                cs = _dot(wblk.astype(BF16), upper)
                parts.append((log2_beta[:, sl] - _lane_tile(c, cb // LANES)) - cs)
                c = c + jnp.broadcast_to(cs[:, 0:1] + wblk[:, 0:1], c.shape)
            c_scr[h] = c
            a = jnp.exp2(jnp.concatenate(parts[::-1], axis=1))
            if diag:
                a = jnp.where(causal, a, 0.0)
            acc_scr[h] += _dot(a.astype(BF16), v)

    n_rest = jnp.maximum(qi - 1, 0)
    odd = n_rest % 2

    @pl.when(qi == 0)
    def _():
        tile(0, True)

    @pl.when((qi >= 1) & (odd == 0))
    def _():
        tile(qi, True)
        tile(qi - 1, False)

    @pl.when(odd == 1)
    def _():
        tile(qi, True)
        tile(qi - 1, False)
        tile(qi - 2, False)

    def body(jp, carry):
        tile(qi - 2 - odd - 2 * jp, False)
        tile(qi - 3 - odd - 2 * jp, False)
        return carry

    lax.fori_loop(0, n_rest // 2, body, 0)

    for h in range(hp):
        o_ref[0, :, h * hd:(h + 1) * hd] = (acc_scr[h] * _silu(z_ref[0, :, h * hd:(h + 1) * hd])).astype(o_ref.dtype)


def stick_breaking_branch(qkv, rest, t, hp):
    b, s, _ = qkv.shape
    t = _tile(s, t)
    hd = SB_HEAD_DIM
    w = hp * hd
    cpb = BRANCH_WIDTH // w
    return pl.pallas_call(
        functools.partial(_sb_kernel, t=t, hp=hp),
        grid=(b, SB_HEADS // hp, s // t),
        in_specs=[
            pl.BlockSpec((1, t, w), lambda bi, h, qi: (bi, qi, QKV_SB_Q * cpb + h)),
            pl.BlockSpec((1, s, w), lambda bi, h, qi: (bi, 0, QKV_SB_K * cpb + h)),
            pl.BlockSpec((1, s, w), lambda bi, h, qi: (bi, 0, QKV_SB_V * cpb + h)),
            pl.BlockSpec((1, t, w), lambda bi, h, qi: (bi, qi, REST_SB_Z * cpb + h)),
        ],
        out_specs=pl.BlockSpec((1, t, w), lambda bi, h, qi: (bi, qi, h)),
        out_shape=jax.ShapeDtypeStruct((b, s, BRANCH_WIDTH), BF16),
        scratch_shapes=[
            pltpu.VMEM((hp, s // t, hd, t), BF16),
            pltpu.VMEM((hp, t, hd), BF16),
            pltpu.VMEM((hp, t, LANES), F32),
            pltpu.VMEM((hp, t, hd), F32),
        ],
        compiler_params=_params("parallel", "parallel", "arbitrary"),
        name="stick_breaking",
    )(qkv, qkv, qkv, rest)


def _pool_kernel(u_ref, z_ref, w_ref, sc_ref, o_ref, ext_scr, *, tt):
    ti = pl.program_id(1)

    @pl.when(ti == 0)
    def _():
        ext_scr[0:POOL_HALO] = jnp.zeros((POOL_HALO, BRANCH_WIDTH), F32)

    ext_scr[POOL_HALO:POOL_HALO + tt] = u_ref[0]
    pos = ti * tt + lax.broadcasted_iota(jnp.int32, (tt, 1), 0)
    for g, w in enumerate(POOL_WINDOWS):
        cols = slice(g * POOL_GROUP, (g + 1) * POOL_GROUP)
        u = ext_scr[POOL_HALO:POOL_HALO + tt, cols]
        win = u
        for j in range(1, w):
            win = win + ext_scr[POOL_HALO - j:POOL_HALO - j + tt, cols]
        count = jnp.minimum(pos + 1, w).astype(F32)
        pooled = win / count - u
        mixed = _dot(pooled.astype(BF16), w_ref[g]) * sc_ref[:, cols]
        o_ref[0, :, cols] = (mixed * _silu(z_ref[0, :, cols])).astype(o_ref.dtype)
    ext_scr[0:POOL_HALO] = ext_scr[tt:tt + POOL_HALO]


def pool_branch(rest, w_pool_bf16, pool_scale, layer, tt=1024):
    b, s, _ = rest.shape
    tt = _tile(s, tt)
    assert tt >= POOL_HALO and max(POOL_WINDOWS) <= POOL_HALO
    return pl.pallas_call(
        functools.partial(_pool_kernel, tt=tt),
        grid=(b, s // tt),
        in_specs=[
            pl.BlockSpec((1, tt, BRANCH_WIDTH), lambda bi, ti: (bi, ti, REST_POOL_U)),
            pl.BlockSpec((1, tt, BRANCH_WIDTH), lambda bi, ti: (bi, ti, REST_POOL_Z)),
            pl.BlockSpec((None,) + w_pool_bf16.shape[1:], lambda bi, ti: (layer, 0, 0, 0)),
            pl.BlockSpec((None, 1, BRANCH_WIDTH), lambda bi, ti: (layer, 0, 0)),
        ],
        out_specs=pl.BlockSpec((1, tt, BRANCH_WIDTH), lambda bi, ti: (bi, ti, 0)),
        out_shape=jax.ShapeDtypeStruct((b, s, BRANCH_WIDTH), BF16),
        scratch_shapes=[pltpu.VMEM((tt + POOL_HALO, BRANCH_WIDTH), F32)],
        compiler_params=_params("parallel", "arbitrary"),
        name="multiscale_pool",
    )(rest, rest, w_pool_bf16, pool_scale.reshape(-1, 1, BRANCH_WIDTH))


def _mem_attn_kernel(q_ref, z_ref, mk_ref, mv_ref, o_ref):
    hd = MEM_HEAD_DIM
    for h in range(MEM_HEADS):
        cols = slice(h * hd, (h + 1) * hd)
        q = q_ref[0, :, cols] * (MEM_HEAD_DIM ** -0.5)
        s = _dot_nt(q, mk_ref[0, :, cols].astype(BF16))
        p = jnp.exp(s - jnp.max(s, axis=-1, keepdims=True))
        l = jnp.sum(p, axis=-1, keepdims=True)
        o = _dot(p.astype(BF16), mv_ref[0, :, cols].astype(BF16)) / l
        o_ref[0, :, cols] = (o * _silu(z_ref[0, :, cols])).astype(o_ref.dtype)


def memory_branch(qkv, rest, mkv, tq=1024):
    b, s, _ = qkv.shape
    m = mkv.shape[1]
    tq = _tile(s, tq)
    w = BRANCH_WIDTH
    return pl.pallas_call(
        _mem_attn_kernel,
        grid=(b, s // tq),
        in_specs=[
            pl.BlockSpec((1, tq, w), lambda bi, qi: (bi, qi, QKV_MEM_Q)),
            pl.BlockSpec((1, tq, w), lambda bi, qi: (bi, qi, REST_MEM_Z)),
            pl.BlockSpec((1, m, w), lambda bi, qi: (bi, 0, 0)),
            pl.BlockSpec((1, m, w), lambda bi, qi: (bi, 0, 1)),
        ],
        out_specs=pl.BlockSpec((1, tq, w), lambda bi, qi: (bi, qi, 0)),
        out_shape=jax.ShapeDtypeStruct((b, s, BRANCH_WIDTH), BF16),
        compiler_params=_params("parallel", "parallel"),
        name="memory_attention",
    )(qkv, rest, mkv, mkv)


def _merge_kernel(b0_ref, b1_ref, b2_ref, b3_ref, g0_ref, g1_ref, g2_ref, g3_ref, gb_ref, w_ref, o_ref):
    branches = (b0_ref, b1_ref, b2_ref, b3_ref)
    gates = (g0_ref, g1_ref, g2_ref, g3_ref)
    merged = None
    for n in range(N_BRANCH):
        gate = 1.0 / (1.0 + jnp.exp(-(gates[n][...] + gb_ref[n:n + 1, :])))
        term = gate * _dot(branches[n][...], w_ref[n])
        merged = term if merged is None else merged + term
    o_ref[...] = merged.astype(o_ref.dtype)


def gated_merge(branches, rest2d, gate_b, w_branch_bf16, layer, tm=512, tn=1024):
    m = rest2d.shape[0]
    d = w_branch_bf16.shape[3]
    tm, tn = _tile(m, tm), _tile(d, tn)
    gate_col0 = REST_GATES * BRANCH_WIDTH // tn
    per_gate = d // tn
    br_spec = pl.BlockSpec((tm, BRANCH_WIDTH), lambda j, i: (i, 0))

    def gate_spec(n):
        return pl.BlockSpec((tm, tn), lambda j, i: (i, gate_col0 + n * per_gate + j))

    return pl.pallas_call(
        _merge_kernel,
        grid=(d // tn, m // tm),
        in_specs=[br_spec] * N_BRANCH + [gate_spec(n) for n in range(N_BRANCH)] + [
            pl.BlockSpec((None, N_BRANCH, tn), lambda j, i: (layer, 0, j)),
            pl.BlockSpec((None, N_BRANCH, BRANCH_WIDTH, tn), lambda j, i: (layer, 0, 0, j)),
        ],
        out_specs=pl.BlockSpec((tm, tn), lambda j, i: (i, j)),
        out_shape=jax.ShapeDtypeStruct((m, d), BF16),
        compiler_params=_params("parallel", "parallel"),
        name="gated_merge",
    )(*branches, rest2d, rest2d, rest2d, rest2d, gate_b, w_branch_bf16)


def kernel(x, mem, rel_bias, norm_g, w_in, gate_b, lam_q1, lam_k1, lam_q2, lam_k2, da_norm_g, w_pool,
           pool_scale, mem_norm_g, w_mem_kv, w_branch, w_out, final_g):
    b, s, d = x.shape
    depth = norm_g.shape[0]
    n_mem = mem.shape[1]
    t = min(ATTN_TILE, s)
    hp = ATTN_HEADS_PER_STEP

    w_in_bf, w_pool_bf, w_mem_kv_bf = w_in.astype(BF16), w_pool.astype(BF16), w_mem_kv.astype(BF16)
    w_branch_bf, w_out_bf = w_branch.astype(BF16), w_out.astype(BF16)
    lamv = jnp.stack([lam_q1, lam_k1, lam_q2, lam_k2], axis=1)
    n_gate_blocks = (w_in.shape[2] - N_SLICES * BRANCH_WIDTH) // BRANCH_WIDTH
    rest_blocks = REST_SLICES + tuple(range(N_SLICES, N_SLICES + n_gate_blocks))
    mkv_blocks = tuple(range(w_mem_kv.shape[2] // BRANCH_WIDTH))

    bias = bias_tiles(rel_bias, t)
    x2d = x.reshape(b * s, d)
    mem2d = mem.reshape(b * n_mem, d)
    h = rmsnorm_bf16(x2d, norm_g, 0)
    for l in range(depth):
        lam_init = 0.8 - 0.6 * math.exp(-0.3 * l)
        qkv = matmul_cols(h, w_in_bf, l, QKV_SLICES, BF16).reshape(b, s, -1)
        rest2d = matmul_cols(h, w_in_bf, l, rest_blocks, F32)
        rest = rest2d.reshape(b, s, -1)

        br_da = diff_attention_branch(qkv, rest, bias, lamv, da_norm_g, l, lam_init, t, hp)
        br_sb = stick_breaking_branch(qkv, rest, t, hp)
        br_pool = pool_branch(rest, w_pool_bf, pool_scale, l)
        mkv = matmul_cols(rmsnorm_bf16(mem2d, mem_norm_g, l), w_mem_kv_bf, l, mkv_blocks, F32)
        br_mem = memory_branch(qkv, rest, mkv.reshape(b, n_mem, 2 * BRANCH_WIDTH))

        branches = [br.reshape(b * s, BRANCH_WIDTH) for br in (br_da, br_sb, br_pool, br_mem)]
        merged = gated_merge(branches, rest2d, gate_b, w_branch_bf, l)
        if l == depth - 1:
            (out2d,) = matmul_residual(merged, w_out_bf, l, x2d, final_g, 0, final_norm=True)
        else:
            x2d, h = matmul_residual(merged, w_out_bf, l, x2d, norm_g, l + 1, final_norm=False)
    return out2d.reshape(b, s, d)
```

```python
import functools
import math

import jax
import jax.numpy as jnp
from jax import lax
from jax.experimental import pallas as pl
from jax.experimental.pallas import tpu as pltpu

F32 = jnp.float32
BF16 = jnp.bfloat16

BRANCH_WIDTH = 1024
N_BRANCH = 4
N_SLICES = 12
DA_HEADS = 8
DA_QK_DIM = 64
DA_V_DIM = 2 * DA_QK_DIM
SB_HEADS = 8
SB_HEAD_DIM = BRANCH_WIDTH // SB_HEADS
POOL_WINDOWS = (2, 4, 8, 16)
POOL_GROUP = BRANCH_WIDTH // len(POOL_WINDOWS)
MEM_HEADS = 4
MEM_HEAD_DIM = BRANCH_WIDTH // MEM_HEADS
REL_BUCKETS = 32
REL_MAX_DIST = 128
EPS = 1e-6

SL_DA_Q, SL_DA_K, SL_DA_V, SL_DA_Z = 0, 1, 2, 3
SL_SB_Q, SL_SB_K, SL_SB_V, SL_SB_Z = 4, 5, 6, 7
SL_POOL_U, SL_POOL_Z, SL_MEM_Q, SL_MEM_Z = 8, 9, 10, 11
QKV_SLICES = (SL_DA_Q, SL_DA_K, SL_DA_V, SL_SB_Q, SL_SB_K, SL_SB_V, SL_MEM_Q)
REST_SLICES = (SL_DA_Z, SL_SB_Z, SL_POOL_U, SL_POOL_Z, SL_MEM_Z)
QKV_DA_Q, QKV_DA_K, QKV_DA_V, QKV_SB_Q, QKV_SB_K, QKV_SB_V, QKV_MEM_Q = range(len(QKV_SLICES))
REST_DA_Z, REST_SB_Z, REST_POOL_U, REST_POOL_Z, REST_MEM_Z, REST_GATES = range(len(REST_SLICES) + 1)

V7X_VMEM_LIMIT_BYTES = 56 * 1024 * 1024
LANES = 128
POOL_HALO = 16
NEG_BIG = -1e30
ATTN_TILE = 512
ATTN_HEADS_PER_STEP = 2
SB_CUMSUM_BLOCK = 256
LOG2E = 1.4426950408889634
LN2 = 0.6931471805599453


def _params(*sem):
    return pltpu.CompilerParams(dimension_semantics=sem, vmem_limit_bytes=V7X_VMEM_LIMIT_BYTES)


def _tile(n, t):
    t = min(t, n)
    assert n % t == 0, (n, t)
    return t


def _silu(z):
    return z * (1.0 / (1.0 + jnp.exp(-z)))


def _dot_nt(a, b):
    return lax.dot_general(a, b, (((1,), (1,)), ((), ())), preferred_element_type=F32)


def _dot(a, b):
    return jnp.dot(a, b, preferred_element_type=F32)


def _lane_tile(x, n):
    return x if n == 1 else jnp.concatenate([x] * n, axis=1)


def _neg_abs(x):
    u = lax.bitcast_convert_type(x, jnp.uint32) | jnp.uint32(0x80000000)
    return lax.bitcast_convert_type(u, F32)


def _rmsnorm_kernel(x_ref, g_ref, o_ref):
    x = x_ref[...]
    y = x * lax.rsqrt(jnp.mean(x * x, axis=-1, keepdims=True) + EPS)
    o_ref[...] = (y * g_ref[...]).astype(o_ref.dtype)


def rmsnorm_bf16(x2d, gains, layer, tr=512):
    m, d = x2d.shape
    tr = _tile(m, tr)
    return pl.pallas_call(
        _rmsnorm_kernel,
        grid=(m // tr,),
        in_specs=[
            pl.BlockSpec((tr, d), lambda i: (i, 0)),
            pl.BlockSpec((None, 1, d), lambda i: (layer, 0, 0)),
        ],
        out_specs=pl.BlockSpec((tr, d), lambda i: (i, 0)),
        out_shape=jax.ShapeDtypeStruct((m, d), BF16),
        compiler_params=_params("parallel"),
        name="rmsnorm",
    )(x2d, gains.reshape(-1, 1, d))


def _matmul_kernel(h_ref, w_ref, o_ref):
    o_ref[...] = _dot(h_ref[...], w_ref[...]).astype(o_ref.dtype)


def _static_lookup(table, j):
    r = jnp.int32(table[0])
    for idx in range(1, len(table)):
        r = jnp.where(j == idx, jnp.int32(table[idx]), r)
    return r


def matmul_cols(h, w, layer, col_blocks, out_dtype, tm=2048, tn=BRANCH_WIDTH):
    m, d = h.shape
    tm = _tile(m, tm)
    assert w.shape[2] % tn == 0
    n_out = len(col_blocks)
    return pl.pallas_call(
        _matmul_kernel,
        grid=(m // tm, n_out),
        in_specs=[
            pl.BlockSpec((tm, d), lambda i, j: (i, 0)),
            pl.BlockSpec((None, d, tn), lambda i, j: (layer, 0, _static_lookup(col_blocks, j))),
        ],
        out_specs=pl.BlockSpec((tm, tn), lambda i, j: (i, j)),
        out_shape=jax.ShapeDtypeStruct((m, n_out * tn), out_dtype),
        compiler_params=_params("parallel", "parallel"),
        name="in_proj",
    )(h, w)


def _matmul_residual_kernel(a_ref, w_ref, r_ref, g_ref, *out_refs, final_norm):
    y = r_ref[...] + _dot(a_ref[...], w_ref[...])
    normed = (y * lax.rsqrt(jnp.mean(y * y, axis=-1, keepdims=True) + EPS)) * g_ref[...]
    if final_norm:
        out_refs[0][...] = normed
    else:
        out_refs[0][...] = y
        out_refs[1][...] = normed.astype(BF16)


def matmul_residual(a_bf16, w_bf16, layer, res, gains, gain_index, final_norm, tm=512):
    m, k = a_bf16.shape
    n = w_bf16.shape[2]
    tm = _tile(m, tm)
    row_spec = pl.BlockSpec((tm, n), lambda i: (i, 0))
    out_shapes = [jax.ShapeDtypeStruct((m, n), F32)]
    if not final_norm:
        out_shapes.append(jax.ShapeDtypeStruct((m, n), BF16))
    return pl.pallas_call(
        functools.partial(_matmul_residual_kernel, final_norm=final_norm),
        grid=(m // tm,),
        in_specs=[
            pl.BlockSpec((tm, k), lambda i: (i, 0)),
            pl.BlockSpec((None, k, n), lambda i: (layer, 0, 0)),
            row_spec,
            pl.BlockSpec((None, 1, n), lambda i: (gain_index, 0, 0)),
        ],
        out_specs=[row_spec] * len(out_shapes),
        out_shape=out_shapes,
        compiler_params=_params("parallel"),
        name="out_proj_residual",
    )(a_bf16, w_bf16, res, gains.reshape(-1, 1, n))


def _bias_tiles_kernel(rb_ref, o_ref, *, t):
    h = pl.program_id(0)
    blk = REL_MAX_DIST
    nb = t // blk
    qi = lax.broadcasted_iota(jnp.int32, (blk, blk), 0)
    ki = lax.broadcasted_iota(jnp.int32, (blk, blk), 1)
    max_exact = REL_BUCKETS // 2
    far = rb_ref[REL_BUCKETS - 1, h]

    def band(offset):
        n = jnp.maximum(qi - ki + offset, 0)
        nf = jnp.maximum(n, 1).astype(F32)
        large = max_exact + (jnp.log(nf / max_exact) / math.log(REL_MAX_DIST / max_exact)
                             * (REL_BUCKETS - max_exact)).astype(jnp.int32)
        large = jnp.minimum(large, REL_BUCKETS - 1)
        bucket = jnp.where(n < max_exact, n, large)
        val = jnp.zeros((blk, blk), F32)
        for b in range(REL_BUCKETS - 1):
            val = jnp.where(bucket == b, rb_ref[b, h] - far, val)
        return val

    on_diag = jnp.where(qi >= ki, band(0), NEG_BIG)
    below = band(blk)
    zeros = jnp.zeros((blk, blk), F32)
    masked = jnp.full((blk, blk), NEG_BIG, F32)
    for bi in range(nb):
        for bj in range(nb):
            rows, cols = slice(bi * blk, (bi + 1) * blk), slice(bj * blk, (bj + 1) * blk)
            o_ref[0, 0, rows, cols] = (on_diag if bj == bi else below if bj == bi - 1
                                       else masked if bj > bi else zeros)
            o_ref[0, 1, rows, cols] = below if (bi == 0 and bj == nb - 1) else zeros


def bias_tiles(rel_bias, t):
    assert t % REL_MAX_DIST == 0
    heads = rel_bias.shape[1]
    return pl.pallas_call(
        functools.partial(_bias_tiles_kernel, t=t),
        grid=(heads,),
        in_specs=[pl.BlockSpec(memory_space=pltpu.SMEM)],
        out_specs=pl.BlockSpec((1, 2, t, t), lambda h: (h, 0, 0, 0)),
        out_shape=jax.ShapeDtypeStruct((heads, 2, t, t), F32),
        compiler_params=_params("parallel"),
        name="t5_bias_tiles",
    )(rel_bias)


def _attn_kernel(lamv_ref, dq_ref, dk_ref, dv_ref, dz_ref, bias_ref, g_ref,
                 sq_ref, sk_ref, sv_ref, sz_ref, do_ref, so_ref,
                 dkt_scr, dv_scr, dq_scr, m_scr, dacc_scr, skt_scr, sq_scr, c_scr, sacc_scr,
                 *, t, hp, lam_init):
    qi = pl.program_id(2)
    nt = dkt_scr.shape[1]
    hd = DA_V_DIM
    cb = min(SB_CUMSUM_BLOCK, t)

    @pl.when(qi == 0)
    def _():
        for h in range(hp):
            for jj in range(nt):
                dkt_scr[h, jj] = dk_ref[0, jj * t:(jj + 1) * t, h * hd:(h + 1) * hd].astype(F32).T.astype(BF16)
                skt_scr[h, jj] = sk_ref[0, jj * t:(jj + 1) * t, h * hd:(h + 1) * hd].astype(F32).T.astype(BF16)
            dv_scr[h, :, 0:hd] = dv_ref[0, :, h * hd:(h + 1) * hd]
            dv_scr[h, :, hd:2 * hd] = jnp.ones((dv_scr.shape[1], hd), BF16)

    lane = lax.broadcasted_iota(jnp.int32, (t, hd), 1)
    for h in range(hp):
        q = dq_ref[0, :, h * hd:(h + 1) * hd] * (DA_QK_DIM ** -0.5)
        dq_scr[h, 0] = jnp.where(lane < DA_QK_DIM, q, 0.0).astype(BF16)
        dq_scr[h, 1] = jnp.where(lane >= DA_QK_DIM, q, 0.0).astype(BF16)
        sq_scr[h] = (sq_ref[0, :, h * hd:(h + 1) * hd].astype(F32) * (SB_HEAD_DIM ** -0.5 * LOG2E)).astype(BF16)
    m_scr[...] = jnp.full(m_scr.shape, -jnp.inf, F32)
    dacc_scr[...] = jnp.zeros(dacc_scr.shape, F32)
    c_scr[...] = jnp.zeros(c_scr.shape, F32)
    sacc_scr[...] = jnp.zeros(sacc_scr.shape, F32)

    def da_tile(j, bias_sel):
        off = pl.multiple_of(j * t, t)
        for h in range(hp):
            kt = dkt_scr[h, j]
            v = dv_scr[h, pl.ds(off, t), :]
            for c in range(2):
                s = _dot(dq_scr[h, c], kt)
                if bias_sel is not None:
                    s = s + bias_ref[h, bias_sel]
                m_prev = m_scr[h, c]
                m_new = jnp.maximum(m_prev, jnp.max(s, axis=-1, keepdims=True))
                alpha = jnp.exp(m_prev - m_new)
                p = jnp.exp(s - _lane_tile(m_new, t // LANES))
                dacc_scr[h, c] = _lane_tile(alpha, 2) * dacc_scr[h, c] + _dot(p.astype(BF16), v)
                m_scr[h, c] = m_new

    def sb_tile(j, diag):
        off = pl.multiple_of(j * t, t)
        r2 = lax.broadcasted_iota(jnp.int32, (cb, cb), 0)
        c2 = lax.broadcasted_iota(jnp.int32, (cb, cb), 1)
        upper = jnp.where(r2 > c2, 1.0, 0.0).astype(BF16)
        if diag:
            row = lax.broadcasted_iota(jnp.int32, (t, t), 0)
            col = lax.broadcasted_iota(jnp.int32, (t, t), 1)
            causal = col < row
        for h in range(hp):
            v = sv_ref[0, pl.ds(off, t), h * hd:(h + 1) * hd]
            z2 = _dot(sq_scr[h], skt_scr[h, j])
            w = jnp.maximum(z2, 0.0) + jnp.log(1.0 + jnp.exp2(_neg_abs(z2))) * (1.0 / LN2)
            log2_beta = z2 - w
            if diag:
                w = jnp.where(causal, w, 0.0)
            c = c_scr[h]
            parts = []
            for blk in reversed(range(t // cb)):
                sl = slice(blk * cb, (blk + 1) * cb)
                wblk = w[:, sl]
                cs = _dot(wblk.astype(BF16), upper)
                parts.append((log2_beta[:, sl] - _lane_tile(c, cb // LANES)) - cs)
                c = c + jnp.broadcast_to(cs[:, 0:1] + wblk[:, 0:1], c.shape)
            c_scr[h] = c
            a = jnp.exp2(jnp.concatenate(parts[::-1], axis=1))
            if diag:
                a = jnp.where(causal, a, 0.0)
            sacc_scr[h] += _dot(a.astype(BF16), v)

    @pl.when(qi == 0)
    def _():
        da_tile(0, 0)
        sb_tile(0, True)

    @pl.when(qi == 1)
    def _():
        da_tile(0, 1)
        sb_tile(1, True)
        da_tile(1, 0)
        sb_tile(0, False)

    @pl.when(qi >= 2)
    def _():
        da_tile(0, None)
        sb_tile(qi, True)

        def body(j, carry):
            da_tile(j, None)
            sb_tile(qi - j, False)
            return carry

        lax.fori_loop(1, qi - 1, body, 0)
        da_tile(qi - 1, 1)
        sb_tile(1, False)
        da_tile(qi, 0)
        sb_tile(0, False)

    lv = lamv_ref[...]
    s1 = jnp.sum(lv[0:1] * lv[1:2], axis=-1, keepdims=True)
    s2 = jnp.sum(lv[2:3] * lv[3:4], axis=-1, keepdims=True)
    lam = jnp.exp(s1) - jnp.exp(s2) + lam_init
    for h in range(hp):
        cols = slice(h * hd, (h + 1) * hd)
        a0 = dacc_scr[h, 0]
        a1 = dacc_scr[h, 1]
        o = a0[:, 0:hd] / a0[:, hd:2 * hd] - lam * (a1[:, 0:hd] / a1[:, hd:2 * hd])
        y = o * lax.rsqrt(jnp.mean(o * o, axis=-1, keepdims=True) + EPS)
        y = (y * g_ref[:, cols]) * (1.0 - lam_init)
        do_ref[0, :, cols] = (y * _silu(dz_ref[0, :, cols])).astype(do_ref.dtype)
        so_ref[0, :, cols] = (sacc_scr[h] * _silu(sz_ref[0, :, cols])).astype(so_ref.dtype)


def attention_branches(qkv, rest, bias, lamv, da_norm_g, layer, lam_init, t, hp):
    b, s, _ = qkv.shape
    t = _tile(s, t)
    hd = DA_V_DIM
    w = hp * hd
    cpb = BRANCH_WIDTH // w
    q_spec = lambda sl: pl.BlockSpec((1, t, w), lambda bi, h, qi: (bi, qi, sl * cpb + h))
    kv_spec = lambda sl: pl.BlockSpec((1, s, w), lambda bi, h, qi: (bi, 0, sl * cpb + h))
    out_spec = pl.BlockSpec((1, t, w), lambda bi, h, qi: (bi, qi, h))
    return pl.pallas_call(
        functools.partial(_attn_kernel, t=t, hp=hp, lam_init=lam_init),
        grid=(b, DA_HEADS // hp, s // t),
        in_specs=[
            pl.BlockSpec((None, 4, DA_QK_DIM), lambda bi, h, qi: (layer, 0, 0)),
            q_spec(QKV_DA_Q), kv_spec(QKV_DA_K), kv_spec(QKV_DA_V), q_spec(REST_DA_Z),
            pl.BlockSpec((hp, 2, t, t), lambda bi, h, qi: (h, 0, 0, 0)),
            pl.BlockSpec((None, 1, w), lambda bi, h, qi: (layer, 0, h)),
            q_spec(QKV_SB_Q), kv_spec(QKV_SB_K), kv_spec(QKV_SB_V), q_spec(REST_SB_Z),
        ],
        out_specs=[out_spec, out_spec],
        out_shape=[jax.ShapeDtypeStruct((b, s, BRANCH_WIDTH), BF16)] * 2,
        scratch_shapes=[
            pltpu.VMEM((hp, s // t, hd, t), BF16),
            pltpu.VMEM((hp, s, 2 * hd), BF16),
            pltpu.VMEM((hp, 2, t, hd), BF16),
            pltpu.VMEM((hp, 2, t, LANES), F32),
            pltpu.VMEM((hp, 2, t, 2 * hd), F32),
            pltpu.VMEM((hp, s // t, hd, t), BF16),
            pltpu.VMEM((hp, t, hd), BF16),
            pltpu.VMEM((hp, t, LANES), F32),
            pltpu.VMEM((hp, t, hd), F32),
        ],
        compiler_params=_params("parallel", "parallel", "arbitrary"),
        name="causal_attention",
    )(lamv, qkv, qkv, qkv, rest, bias, da_norm_g.reshape(-1, 1, BRANCH_WIDTH), qkv, qkv, qkv, rest)


def _pool_kernel(u_ref, z_ref, w_ref, sc_ref, o_ref, ext_scr, *, tt):
    ti = pl.program_id(1)

    @pl.when(ti == 0)
    def _():
        ext_scr[0:POOL_HALO] = jnp.zeros((POOL_HALO, BRANCH_WIDTH), F32)

    ext_scr[POOL_HALO:POOL_HALO + tt] = u_ref[0]
    pos = ti * tt + lax.broadcasted_iota(jnp.int32, (tt, 1), 0)
    for g, w in enumerate(POOL_WINDOWS):
        cols = slice(g * POOL_GROUP, (g + 1) * POOL_GROUP)
        u = ext_scr[POOL_HALO:POOL_HALO + tt, cols]
        win = u
        for j in range(1, w):
            win = win + ext_scr[POOL_HALO - j:POOL_HALO - j + tt, cols]
        count = jnp.minimum(pos + 1, w).astype(F32)
        pooled = win / count - u
        mixed = _dot(pooled.astype(BF16), w_ref[g]) * sc_ref[:, cols]
        o_ref[0, :, cols] = (mixed * _silu(z_ref[0, :, cols])).astype(o_ref.dtype)
    ext_scr[0:POOL_HALO] = ext_scr[tt:tt + POOL_HALO]


def pool_branch(rest, w_pool_bf16, pool_scale, layer, tt=1024):
    b, s, _ = rest.shape
    tt = _tile(s, tt)
    assert tt >= POOL_HALO and max(POOL_WINDOWS) <= POOL_HALO
    return pl.pallas_call(
        functools.partial(_pool_kernel, tt=tt),
        grid=(b, s // tt),
        in_specs=[
            pl.BlockSpec((1, tt, BRANCH_WIDTH), lambda bi, ti: (bi, ti, REST_POOL_U)),
            pl.BlockSpec((1, tt, BRANCH_WIDTH), lambda bi, ti: (bi, ti, REST_POOL_Z)),
            pl.BlockSpec((None,) + w_pool_bf16.shape[1:], lambda bi, ti: (layer, 0, 0, 0)),
            pl.BlockSpec((None, 1, BRANCH_WIDTH), lambda bi, ti: (layer, 0, 0)),
        ],
        out_specs=pl.BlockSpec((1, tt, BRANCH_WIDTH), lambda bi, ti: (bi, ti, 0)),
        out_shape=jax.ShapeDtypeStruct((b, s, BRANCH_WIDTH), BF16),
        scratch_shapes=[pltpu.VMEM((tt + POOL_HALO, BRANCH_WIDTH), F32)],
        compiler_params=_params("parallel", "arbitrary"),
        name="multiscale_pool",
    )(rest, rest, w_pool_bf16, pool_scale.reshape(-1, 1, BRANCH_WIDTH))


def _mem_attn_kernel(q_ref, z_ref, mk_ref, mv_ref, o_ref):
    hd = MEM_HEAD_DIM
    for h in range(MEM_HEADS):
        cols = slice(h * hd, (h + 1) * hd)
        q = q_ref[0, :, cols] * (MEM_HEAD_DIM ** -0.5)
        s = _dot_nt(q, mk_ref[0, :, cols].astype(BF16))
        p = jnp.exp(s - jnp.max(s, axis=-1, keepdims=True))
        l = jnp.sum(p, axis=-1, keepdims=True)
        o = _dot(p.astype(BF16), mv_ref[0, :, cols].astype(BF16)) / l
        o_ref[0, :, cols] = (o * _silu(z_ref[0, :, cols])).astype(o_ref.dtype)


def memory_branch(qkv, rest, mkv, tq=1024):
    b, s, _ = qkv.shape
    m = mkv.shape[1]
    tq = _tile(s, tq)
    w = BRANCH_WIDTH
    return pl.pallas_call(
        _mem_attn_kernel,
        grid=(b, s // tq),
        in_specs=[
            pl.BlockSpec((1, tq, w), lambda bi, qi: (bi, qi, QKV_MEM_Q)),
            pl.BlockSpec((1, tq, w), lambda bi, qi: (bi, qi, REST_MEM_Z)),
            pl.BlockSpec((1, m, w), lambda bi, qi: (bi, 0, 0)),
            pl.BlockSpec((1, m, w), lambda bi, qi: (bi, 0, 1)),
        ],
        out_specs=pl.BlockSpec((1, tq, w), lambda bi, qi: (bi, qi, 0)),
        out_shape=jax.ShapeDtypeStruct((b, s, BRANCH_WIDTH), BF16),
        compiler_params=_params("parallel", "parallel"),
        name="memory_attention",
    )(qkv, rest, mkv, mkv)


def _merge_kernel(b0_ref, b1_ref, b2_ref, b3_ref, g0_ref, g1_ref, g2_ref, g3_ref, gb_ref, w_ref, o_ref):
    branches = (b0_ref, b1_ref, b2_ref, b3_ref)
    gates = (g0_ref, g1_ref, g2_ref, g3_ref)
    merged = None
    for n in range(N_BRANCH):
        gate = 1.0 / (1.0 + jnp.exp(-(gates[n][...] + gb_ref[n:n + 1, :])))
        term = gate * _dot(branches[n][...], w_ref[n])
        merged = term if merged is None else merged + term
    o_ref[...] = merged.astype(o_ref.dtype)


def gated_merge(branches, rest2d, gate_b, w_branch_bf16, layer, tm=512, tn=1024):
    m = rest2d.shape[0]
    d = w_branch_bf16.shape[3]
    tm, tn = _tile(m, tm), _tile(d, tn)
    gate_col0 = REST_GATES * BRANCH_WIDTH // tn
    per_gate = d // tn
    br_spec = pl.BlockSpec((tm, BRANCH_WIDTH), lambda j, i: (i, 0))

    def gate_spec(n):
        return pl.BlockSpec((tm, tn), lambda j, i: (i, gate_col0 + n * per_gate + j))

    return pl.pallas_call(
        _merge_kernel,
        grid=(d // tn, m // tm),
        in_specs=[br_spec] * N_BRANCH + [gate_spec(n) for n in range(N_BRANCH)] + [
            pl.BlockSpec((None, N_BRANCH, tn), lambda j, i: (layer, 0, j)),
            pl.BlockSpec((None, N_BRANCH, BRANCH_WIDTH, tn), lambda j, i: (layer, 0, 0, j)),
        ],
        out_specs=pl.BlockSpec((tm, tn), lambda j, i: (i, j)),
        out_shape=jax.ShapeDtypeStruct((m, d), BF16),
        compiler_params=_params("parallel", "parallel"),
        name="gated_merge",
    )(*branches, rest2d, rest2d, rest2d, rest2d, gate_b, w_branch_bf16)


def kernel(x, mem, rel_bias, norm_g, w_in, gate_b, lam_q1, lam_k1, lam_q2, lam_k2, da_norm_g, w_pool,
           pool_scale, mem_norm_g, w_mem_kv, w_branch, w_out, final_g):
    b, s, d = x.shape
    depth = norm_g.shape[0]
    n_mem = mem.shape[1]
    t = min(ATTN_TILE, s)
    hp = ATTN_HEADS_PER_STEP

    w_in_bf, w_pool_bf, w_mem_kv_bf = w_in.astype(BF16), w_pool.astype(BF16), w_mem_kv.astype(BF16)
    w_branch_bf, w_out_bf = w_branch.astype(BF16), w_out.astype(BF16)
    lamv = jnp.stack([lam_q1, lam_k1, lam_q2, lam_k2], axis=1)
    n_gate_blocks = (w_in.shape[2] - N_SLICES * BRANCH_WIDTH) // BRANCH_WIDTH
    rest_blocks = REST_SLICES + tuple(range(N_SLICES, N_SLICES + n_gate_blocks))
    mkv_blocks = tuple(range(w_mem_kv.shape[2] // BRANCH_WIDTH))

    bias = bias_tiles(rel_bias, t)
    x2d = x.reshape(b * s, d)
    mem2d = mem.reshape(b * n_mem, d)
    h = rmsnorm_bf16(x2d, norm_g, 0)
    for l in range(depth):
        lam_init = 0.8 - 0.6 * math.exp(-0.3 * l)
        qkv = matmul_cols(h, w_in_bf, l, QKV_SLICES, BF16).reshape(b, s, -1)
        rest2d = matmul_cols(h, w_in_bf, l, rest_blocks, F32)
        rest = rest2d.reshape(b, s, -1)

        br_da, br_sb = attention_branches(qkv, rest, bias, lamv, da_norm_g, l, lam_init, t, hp)
        br_pool = pool_branch(rest, w_pool_bf, pool_scale, l)
        mkv = matmul_cols(rmsnorm_bf16(mem2d, mem_norm_g, l), w_mem_kv_bf, l, mkv_blocks, F32)
        br_mem = memory_branch(qkv, rest, mkv.reshape(b, n_mem, 2 * BRANCH_WIDTH))

        branches = [br.reshape(b * s, BRANCH_WIDTH) for br in (br_da, br_sb, br_pool, br_mem)]
        merged = gated_merge(branches, rest2d, gate_b, w_branch_bf, l)
        if l == depth - 1:
            (out2d,) = matmul_residual(merged, w_out_bf, l, x2d, final_g, 0, final_norm=True)
        else:
            x2d, h = matmul_residual(merged, w_out_bf, l, x2d, norm_g, l + 1, final_norm=False)
    return out2d.reshape(b, s, d)
```

```python
import functools
import math

import jax
import jax.numpy as jnp
from jax import lax
from jax.experimental import pallas as pl
from jax.experimental.pallas import tpu as pltpu

F32 = jnp.float32
BF16 = jnp.bfloat16

BRANCH_WIDTH = 1024
N_BRANCH = 4
N_SLICES = 12
DA_HEADS = 8
DA_QK_DIM = 64
DA_V_DIM = 2 * DA_QK_DIM
SB_HEADS = 8
SB_HEAD_DIM = BRANCH_WIDTH // SB_HEADS
POOL_WINDOWS = (2, 4, 8, 16)
POOL_GROUP = BRANCH_WIDTH // len(POOL_WINDOWS)
MEM_HEADS = 4
MEM_HEAD_DIM = BRANCH_WIDTH // MEM_HEADS
REL_BUCKETS = 32
REL_MAX_DIST = 128
EPS = 1e-6

SL_DA_Q, SL_DA_K, SL_DA_V, SL_DA_Z = 0, 1, 2, 3
SL_SB_Q, SL_SB_K, SL_SB_V, SL_SB_Z = 4, 5, 6, 7
SL_POOL_U, SL_POOL_Z, SL_MEM_Q, SL_MEM_Z = 8, 9, 10, 11
QKV_SLICES = (SL_DA_Q, SL_DA_K, SL_DA_V, SL_SB_Q, SL_SB_K, SL_SB_V, SL_MEM_Q)
REST_SLICES = (SL_DA_Z, SL_SB_Z, SL_POOL_U, SL_POOL_Z, SL_MEM_Z)
QKV_DA_Q, QKV_DA_K, QKV_DA_V, QKV_SB_Q, QKV_SB_K, QKV_SB_V, QKV_MEM_Q = range(len(QKV_SLICES))
REST_DA_Z, REST_SB_Z, REST_POOL_U, REST_POOL_Z, REST_MEM_Z, REST_GATES = range(len(REST_SLICES) + 1)

V7X_VMEM_LIMIT_BYTES = 56 * 1024 * 1024
LANES = 128
POOL_HALO = 16
NEG_BIG = -1e30
ATTN_TILE = 512
ATTN_HEADS_PER_STEP = 2
SB_CUMSUM_BLOCK = 256
SB_DEAD_LOG2 = 160.0
LOG2E = 1.4426950408889634
LN2 = 0.6931471805599453


def _params(*sem):
    return pltpu.CompilerParams(dimension_semantics=sem, vmem_limit_bytes=V7X_VMEM_LIMIT_BYTES)


def _tile(n, t):
    t = min(t, n)
    assert n % t == 0, (n, t)
    return t


def _silu(z):
    return z * (1.0 / (1.0 + jnp.exp(-z)))


def _dot_nt(a, b):
    return lax.dot_general(a, b, (((1,), (1,)), ((), ())), preferred_element_type=F32)


def _dot(a, b):
    return jnp.dot(a, b, preferred_element_type=F32)


def _lane_tile(x, n):
    return x if n == 1 else jnp.concatenate([x] * n, axis=1)


def _neg_abs(x):
    u = lax.bitcast_convert_type(x, jnp.uint32) | jnp.uint32(0x80000000)
    return lax.bitcast_convert_type(u, F32)


def _rmsnorm_kernel(x_ref, g_ref, o_ref):
    x = x_ref[...]
    y = x * lax.rsqrt(jnp.mean(x * x, axis=-1, keepdims=True) + EPS)
    o_ref[...] = (y * g_ref[...]).astype(o_ref.dtype)


def rmsnorm_bf16(x2d, gains, layer, tr=512):
    m, d = x2d.shape
    tr = _tile(m, tr)
    return pl.pallas_call(
        _rmsnorm_kernel,
        grid=(m // tr,),
        in_specs=[
            pl.BlockSpec((tr, d), lambda i: (i, 0)),
            pl.BlockSpec((None, 1, d), lambda i: (layer, 0, 0)),
        ],
        out_specs=pl.BlockSpec((tr, d), lambda i: (i, 0)),
        out_shape=jax.ShapeDtypeStruct((m, d), BF16),
        compiler_params=_params("parallel"),
        name="rmsnorm",
    )(x2d, gains.reshape(-1, 1, d))


def _matmul_kernel(h_ref, w_ref, o_ref):
    o_ref[...] = _dot(h_ref[...], w_ref[...]).astype(o_ref.dtype)


def _static_lookup(table, j):
    r = jnp.int32(table[0])
    for idx in range(1, len(table)):
        r = jnp.where(j == idx, jnp.int32(table[idx]), r)
    return r


def matmul_cols(h, w, layer, col_blocks, out_dtype, tm=2048, tn=BRANCH_WIDTH):
    m, d = h.shape
    tm = _tile(m, tm)
    assert w.shape[2] % tn == 0
    n_out = len(col_blocks)
    return pl.pallas_call(
        _matmul_kernel,
        grid=(m // tm, n_out),
        in_specs=[
            pl.BlockSpec((tm, d), lambda i, j: (i, 0)),
            pl.BlockSpec((None, d, tn), lambda i, j: (layer, 0, _static_lookup(col_blocks, j))),
        ],
        out_specs=pl.BlockSpec((tm, tn), lambda i, j: (i, j)),
        out_shape=jax.ShapeDtypeStruct((m, n_out * tn), out_dtype),
        compiler_params=_params("parallel", "parallel"),
        name="in_proj",
    )(h, w)


def _matmul_residual_kernel(a_ref, w_ref, r_ref, g_ref, *out_refs, final_norm):
    y = r_ref[...] + _dot(a_ref[...], w_ref[...])
    normed = (y * lax.rsqrt(jnp.mean(y * y, axis=-1, keepdims=True) + EPS)) * g_ref[...]
    if final_norm:
        out_refs[0][...] = normed
    else:
        out_refs[0][...] = y
        out_refs[1][...] = normed.astype(BF16)


def matmul_residual(a_bf16, w_bf16, layer, res, gains, gain_index, final_norm, tm=512):
    m, k = a_bf16.shape
    n = w_bf16.shape[2]
    tm = _tile(m, tm)
    row_spec = pl.BlockSpec((tm, n), lambda i: (i, 0))
    out_shapes = [jax.ShapeDtypeStruct((m, n), F32)]
    if not final_norm:
        out_shapes.append(jax.ShapeDtypeStruct((m, n), BF16))
    return pl.pallas_call(
        functools.partial(_matmul_residual_kernel, final_norm=final_norm),
        grid=(m // tm,),
        in_specs=[
            pl.BlockSpec((tm, k), lambda i: (i, 0)),
            pl.BlockSpec((None, k, n), lambda i: (layer, 0, 0)),
            row_spec,
            pl.BlockSpec((None, 1, n), lambda i: (gain_index, 0, 0)),
        ],
        out_specs=[row_spec] * len(out_shapes),
        out_shape=out_shapes,
        compiler_params=_params("parallel"),
        name="out_proj_residual",
    )(a_bf16, w_bf16, res, gains.reshape(-1, 1, n))


def _bias_tiles_kernel(rb_ref, o_ref, *, t):
    h = pl.program_id(0)
    blk = REL_MAX_DIST
    nb = t // blk
    qi = lax.broadcasted_iota(jnp.int32, (blk, blk), 0)
    ki = lax.broadcasted_iota(jnp.int32, (blk, blk), 1)
    max_exact = REL_BUCKETS // 2
    far = rb_ref[REL_BUCKETS - 1, h]

    def band(offset):
        n = jnp.maximum(qi - ki + offset, 0)
        nf = jnp.maximum(n, 1).astype(F32)
        large = max_exact + (jnp.log(nf / max_exact) / math.log(REL_MAX_DIST / max_exact)
                             * (REL_BUCKETS - max_exact)).astype(jnp.int32)
        large = jnp.minimum(large, REL_BUCKETS - 1)
        bucket = jnp.where(n < max_exact, n, large)
        val = jnp.zeros((blk, blk), F32)
        for b in range(REL_BUCKETS - 1):
            val = jnp.where(bucket == b, rb_ref[b, h] - far, val)
        return val

    on_diag = jnp.where(qi >= ki, band(0), NEG_BIG)
    below = band(blk)
    zeros = jnp.zeros((blk, blk), F32)
    masked = jnp.full((blk, blk), NEG_BIG, F32)
    for bi in range(nb):
        for bj in range(nb):
            rows, cols = slice(bi * blk, (bi + 1) * blk), slice(bj * blk, (bj + 1) * blk)
            o_ref[0, 0, rows, cols] = (on_diag if bj == bi else below if bj == bi - 1
                                       else masked if bj > bi else zeros)
            o_ref[0, 1, rows, cols] = below if (bi == 0 and bj == nb - 1) else zeros


def bias_tiles(rel_bias, t):
    assert t % REL_MAX_DIST == 0
    heads = rel_bias.shape[1]
    return pl.pallas_call(
        functools.partial(_bias_tiles_kernel, t=t),
        grid=(heads,),
        in_specs=[pl.BlockSpec(memory_space=pltpu.SMEM)],
        out_specs=pl.BlockSpec((1, 2, t, t), lambda h: (h, 0, 0, 0)),
        out_shape=jax.ShapeDtypeStruct((heads, 2, t, t), F32),
        compiler_params=_params("parallel"),
        name="t5_bias_tiles",
    )(rel_bias)


def _attn_kernel(lamv_ref, dq_ref, dk_ref, dv_ref, dz_ref, bias_ref, g_ref,
                 sq_ref, sk_ref, sv_ref, sz_ref, do_ref, so_ref,
                 dkt_scr, dv_scr, dq_scr, m_scr, dacc_scr, skt_scr, sq_scr, c_scr, sacc_scr,
                 *, t, hp, lam_init):
    qi = pl.program_id(2)
    nt = dkt_scr.shape[1]
    hd = DA_V_DIM
    cb = min(SB_CUMSUM_BLOCK, t)

    @pl.when(qi == 0)
    def _():
        for h in range(hp):
            for jj in range(nt):
                dkt_scr[h, jj] = dk_ref[0, jj * t:(jj + 1) * t, h * hd:(h + 1) * hd].astype(F32).T.astype(BF16)
                skt_scr[h, jj] = sk_ref[0, jj * t:(jj + 1) * t, h * hd:(h + 1) * hd].astype(F32).T.astype(BF16)
            dv_scr[h, :, 0:hd] = dv_ref[0, :, h * hd:(h + 1) * hd]
            dv_scr[h, :, hd:2 * hd] = jnp.ones((dv_scr.shape[1], hd), BF16)

    lane = lax.broadcasted_iota(jnp.int32, (t, hd), 1)
    for h in range(hp):
        q = dq_ref[0, :, h * hd:(h + 1) * hd] * (DA_QK_DIM ** -0.5)
        dq_scr[h, 0] = jnp.where(lane < DA_QK_DIM, q, 0.0).astype(BF16)
        dq_scr[h, 1] = jnp.where(lane >= DA_QK_DIM, q, 0.0).astype(BF16)
        sq_scr[h] = (sq_ref[0, :, h * hd:(h + 1) * hd].astype(F32) * (SB_HEAD_DIM ** -0.5 * LOG2E)).astype(BF16)
    m_scr[...] = jnp.full(m_scr.shape, -jnp.inf, F32)
    dacc_scr[...] = jnp.zeros(dacc_scr.shape, F32)
    c_scr[...] = jnp.zeros(c_scr.shape, F32)
    sacc_scr[...] = jnp.zeros(sacc_scr.shape, F32)

    def da_tile(j, bias_sel):
        off = pl.multiple_of(j * t, t)
        for h in range(hp):
            kt = dkt_scr[h, j]
            v = dv_scr[h, pl.ds(off, t), :]
            for c in range(2):
                s = _dot(dq_scr[h, c], kt)
                if bias_sel is not None:
                    s = s + bias_ref[h, bias_sel]
                m_prev = m_scr[h, c]
                m_new = jnp.maximum(m_prev, jnp.max(s, axis=-1, keepdims=True))
                alpha = jnp.exp(m_prev - m_new)
                p = jnp.exp(s - _lane_tile(m_new, t // LANES))
                dacc_scr[h, c] = _lane_tile(alpha, 2) * dacc_scr[h, c] + _dot(p.astype(BF16), v)
                m_scr[h, c] = m_new

    def sb_tile(j, diag):
        off = pl.multiple_of(j * t, t)
        r2 = lax.broadcasted_iota(jnp.int32, (cb, cb), 0)
        c2 = lax.broadcasted_iota(jnp.int32, (cb, cb), 1)
        upper = jnp.where(r2 > c2, 1.0, 0.0).astype(BF16)
        if diag:
            row = lax.broadcasted_iota(jnp.int32, (t, t), 0)
            col = lax.broadcasted_iota(jnp.int32, (t, t), 1)
            causal = col < row
        for h in range(hp):
            v = sv_ref[0, pl.ds(off, t), h * hd:(h + 1) * hd]
            z2 = _dot(sq_scr[h], skt_scr[h, j])
            w = jnp.maximum(z2, 0.0) + jnp.log(1.0 + jnp.exp2(_neg_abs(z2))) * (1.0 / LN2)
            log2_beta = z2 - w
            if diag:
                w = jnp.where(causal, w, 0.0)
            c = c_scr[h]
            parts = []
            for blk in reversed(range(t // cb)):
                sl = slice(blk * cb, (blk + 1) * cb)
                wblk = w[:, sl]
                cs = _dot(wblk.astype(BF16), upper)
                parts.append((log2_beta[:, sl] - _lane_tile(c, cb // LANES)) - cs)
                c = c + jnp.broadcast_to(cs[:, 0:1] + wblk[:, 0:1], c.shape)
            c_scr[h] = c
            a = jnp.exp2(jnp.concatenate(parts[::-1], axis=1))
            if diag:
                a = jnp.where(causal, a, 0.0)
            sacc_scr[h] += _dot(a.astype(BF16), v)

    n_far = jnp.maximum(qi - 1, 0)

    def far_body(jp, carry):
        da_tile(2 * jp, None)
        da_tile(2 * jp + 1, None)
        return carry

    lax.fori_loop(0, n_far // 2, far_body, 0)

    @pl.when(n_far % 2 == 1)
    def _():
        da_tile(n_far - 1, None)

    @pl.when(qi == 0)
    def _():
        da_tile(0, 0)
        sb_tile(0, True)

    @pl.when(qi >= 1)
    def _():
        da_tile(qi - 1, 1)
        sb_tile(qi, True)
        da_tile(qi, 0)
        sb_tile(qi - 1, False)

    def min_decay():
        c = c_scr[0]
        for h in range(1, hp):
            c = jnp.minimum(c, c_scr[h])
        return jnp.min(c)

    def alive(carry):
        j, cmin = carry
        return (j >= 0) & (cmin < SB_DEAD_LOG2)

    def sb_body(carry):
        j, _ = carry
        sb_tile(j, False)
        return j - 1, min_decay()

    lax.while_loop(alive, sb_body, (qi - 2, min_decay()))

    lv = lamv_ref[...]
    s1 = jnp.sum(lv[0:1] * lv[1:2], axis=-1, keepdims=True)
    s2 = jnp.sum(lv[2:3] * lv[3:4], axis=-1, keepdims=True)
    lam = jnp.exp(s1) - jnp.exp(s2) + lam_init
    for h in range(hp):
        cols = slice(h * hd, (h + 1) * hd)
        a0 = dacc_scr[h, 0]
        a1 = dacc_scr[h, 1]
        o = a0[:, 0:hd] / a0[:, hd:2 * hd] - lam * (a1[:, 0:hd] / a1[:, hd:2 * hd])
        y = o * lax.rsqrt(jnp.mean(o * o, axis=-1, keepdims=True) + EPS)
        y = (y * g_ref[:, cols]) * (1.0 - lam_init)
        do_ref[0, :, cols] = (y * _silu(dz_ref[0, :, cols])).astype(do_ref.dtype)
        so_ref[0, :, cols] = (sacc_scr[h] * _silu(sz_ref[0, :, cols])).astype(so_ref.dtype)


def attention_branches(qkv, rest, bias, lamv, da_norm_g, layer, lam_init, t, hp):
    b, s, _ = qkv.shape
    t = _tile(s, t)
    hd = DA_V_DIM
    w = hp * hd
    cpb = BRANCH_WIDTH // w
    q_spec = lambda sl: pl.BlockSpec((1, t, w), lambda bi, h, qi: (bi, qi, sl * cpb + h))
    kv_spec = lambda sl: pl.BlockSpec((1, s, w), lambda bi, h, qi: (bi, 0, sl * cpb + h))
    out_spec = pl.BlockSpec((1, t, w), lambda bi, h, qi: (bi, qi, h))
    return pl.pallas_call(
        functools.partial(_attn_kernel, t=t, hp=hp, lam_init=lam_init),
        grid=(b, DA_HEADS // hp, s // t),
        in_specs=[
            pl.BlockSpec((None, 4, DA_QK_DIM), lambda bi, h, qi: (layer, 0, 0)),
            q_spec(QKV_DA_Q), kv_spec(QKV_DA_K), kv_spec(QKV_DA_V), q_spec(REST_DA_Z),
            pl.BlockSpec((hp, 2, t, t), lambda bi, h, qi: (h, 0, 0, 0)),
            pl.BlockSpec((None, 1, w), lambda bi, h, qi: (layer, 0, h)),
            q_spec(QKV_SB_Q), kv_spec(QKV_SB_K), kv_spec(QKV_SB_V), q_spec(REST_SB_Z),
        ],
        out_specs=[out_spec, out_spec],
        out_shape=[jax.ShapeDtypeStruct((b, s, BRANCH_WIDTH), BF16)] * 2,
        scratch_shapes=[
            pltpu.VMEM((hp, s // t, hd, t), BF16),
            pltpu.VMEM((hp, s, 2 * hd), BF16),
            pltpu.VMEM((hp, 2, t, hd), BF16),
            pltpu.VMEM((hp, 2, t, LANES), F32),
            pltpu.VMEM((hp, 2, t, 2 * hd), F32),
            pltpu.VMEM((hp, s // t, hd, t), BF16),
            pltpu.VMEM((hp, t, hd), BF16),
            pltpu.VMEM((hp, t, LANES), F32),
            pltpu.VMEM((hp, t, hd), F32),
        ],
        compiler_params=_params("parallel", "parallel", "arbitrary"),
        name="causal_attention",
    )(lamv, qkv, qkv, qkv, rest, bias, da_norm_g.reshape(-1, 1, BRANCH_WIDTH), qkv, qkv, qkv, rest)


def _pool_kernel(u_ref, z_ref, w_ref, sc_ref, o_ref, ext_scr, *, tt):
    ti = pl.program_id(1)

    @pl.when(ti == 0)
    def _():
        ext_scr[0:POOL_HALO] = jnp.zeros((POOL_HALO, BRANCH_WIDTH), F32)

    ext_scr[POOL_HALO:POOL_HALO + tt] = u_ref[0]
    pos = ti * tt + lax.broadcasted_iota(jnp.int32, (tt, 1), 0)
    for g, w in enumerate(POOL_WINDOWS):
        cols = slice(g * POOL_GROUP, (g + 1) * POOL_GROUP)
        u = ext_scr[POOL_HALO:POOL_HALO + tt, cols]
        win = u
        for j in range(1, w):
            win = win + ext_scr[POOL_HALO - j:POOL_HALO - j + tt, cols]
        count = jnp.minimum(pos + 1, w).astype(F32)
        pooled = win / count - u
        mixed = _dot(pooled.astype(BF16), w_ref[g]) * sc_ref[:, cols]
        o_ref[0, :, cols] = (mixed * _silu(z_ref[0, :, cols])).astype(o_ref.dtype)
    ext_scr[0:POOL_HALO] = ext_scr[tt:tt + POOL_HALO]


def pool_branch(rest, w_pool_bf16, pool_scale, layer, tt=1024):
    b, s, _ = rest.shape
    tt = _tile(s, tt)
    assert tt >= POOL_HALO and max(POOL_WINDOWS) <= POOL_HALO
    return pl.pallas_call(
        functools.partial(_pool_kernel, tt=tt),
        grid=(b, s // tt),
        in_specs=[
            pl.BlockSpec((1, tt, BRANCH_WIDTH), lambda bi, ti: (bi, ti, REST_POOL_U)),
            pl.BlockSpec((1, tt, BRANCH_WIDTH), lambda bi, ti: (bi, ti, REST_POOL_Z)),
            pl.BlockSpec((None,) + w_pool_bf16.shape[1:], lambda bi, ti: (layer, 0, 0, 0)),
            pl.BlockSpec((None, 1, BRANCH_WIDTH), lambda bi, ti: (layer, 0, 0)),
        ],
        out_specs=pl.BlockSpec((1, tt, BRANCH_WIDTH), lambda bi, ti: (bi, ti, 0)),
        out_shape=jax.ShapeDtypeStruct((b, s, BRANCH_WIDTH), BF16),
        scratch_shapes=[pltpu.VMEM((tt + POOL_HALO, BRANCH_WIDTH), F32)],
        compiler_params=_params("parallel", "arbitrary"),
        name="multiscale_pool",
    )(rest, rest, w_pool_bf16, pool_scale.reshape(-1, 1, BRANCH_WIDTH))


def _mem_attn_kernel(q_ref, z_ref, mk_ref, mv_ref, o_ref):
    hd = MEM_HEAD_DIM
    for h in range(MEM_HEADS):
        cols = slice(h * hd, (h + 1) * hd)
        q = q_ref[0, :, cols] * (MEM_HEAD_DIM ** -0.5)
        s = _dot_nt(q, mk_ref[0, :, cols].astype(BF16))
        p = jnp.exp(s - jnp.max(s, axis=-1, keepdims=True))
        l = jnp.sum(p, axis=-1, keepdims=True)
        o = _dot(p.astype(BF16), mv_ref[0, :, cols].astype(BF16)) / l
        o_ref[0, :, cols] = (o * _silu(z_ref[0, :, cols])).astype(o_ref.dtype)


def memory_branch(qkv, rest, mkv, tq=1024):
    b, s, _ = qkv.shape
    m = mkv.shape[1]
    tq = _tile(s, tq)
    w = BRANCH_WIDTH
    return pl.pallas_call(
        _mem_attn_kernel,
        grid=(b, s // tq),
        in_specs=[
            pl.BlockSpec((1, tq, w), lambda bi, qi: (bi, qi, QKV_MEM_Q)),
            pl.BlockSpec((1, tq, w), lambda bi, qi: (bi, qi, REST_MEM_Z)),
            pl.BlockSpec((1, m, w), lambda bi, qi: (bi, 0, 0)),
            pl.BlockSpec((1, m, w), lambda bi, qi: (bi, 0, 1)),
        ],
        out_specs=pl.BlockSpec((1, tq, w), lambda bi, qi: (bi, qi, 0)),
        out_shape=jax.ShapeDtypeStruct((b, s, BRANCH_WIDTH), BF16),
        compiler_params=_params("parallel", "parallel"),
        name="memory_attention",
    )(qkv, rest, mkv, mkv)


def _merge_kernel(b0_ref, b1_ref, b2_ref, b3_ref, g0_ref, g1_ref, g2_ref, g3_ref, gb_ref, w_ref, o_ref):
    branches = (b0_ref, b1_ref, b2_ref, b3_ref)
    gates = (g0_ref, g1_ref, g2_ref, g3_ref)
    merged = None
    for n in range(N_BRANCH):
        gate = 1.0 / (1.0 + jnp.exp(-(gates[n][...] + gb_ref[n:n + 1, :])))
        term = gate * _dot(branches[n][...], w_ref[n])
        merged = term if merged is None else merged + term
    o_ref[...] = merged.astype(o_ref.dtype)


def gated_merge(branches, rest2d, gate_b, w_branch_bf16, layer, tm=512, tn=1024):
    m = rest2d.shape[0]
    d = w_branch_bf16.shape[3]
    tm, tn = _tile(m, tm), _tile(d, tn)
    gate_col0 = REST_GATES * BRANCH_WIDTH // tn
    per_gate = d // tn
    br_spec = pl.BlockSpec((tm, BRANCH_WIDTH), lambda j, i: (i, 0))

    def gate_spec(n):
        return pl.BlockSpec((tm, tn), lambda j, i: (i, gate_col0 + n * per_gate + j))

    return pl.pallas_call(
        _merge_kernel,
        grid=(d // tn, m // tm),
        in_specs=[br_spec] * N_BRANCH + [gate_spec(n) for n in range(N_BRANCH)] + [
            pl.BlockSpec((None, N_BRANCH, tn), lambda j, i: (layer, 0, j)),
            pl.BlockSpec((None, N_BRANCH, BRANCH_WIDTH, tn), lambda j, i: (layer, 0, 0, j)),
        ],
        out_specs=pl.BlockSpec((tm, tn), lambda j, i: (i, j)),
        out_shape=jax.ShapeDtypeStruct((m, d), BF16),
        compiler_params=_params("parallel", "parallel"),
        name="gated_merge",
    )(*branches, rest2d, rest2d, rest2d, rest2d, gate_b, w_branch_bf16)


def kernel(x, mem, rel_bias, norm_g, w_in, gate_b, lam_q1, lam_k1, lam_q2, lam_k2, da_norm_g, w_pool,
           pool_scale, mem_norm_g, w_mem_kv, w_branch, w_out, final_g):
    b, s, d = x.shape
    depth = norm_g.shape[0]
    n_mem = mem.shape[1]
    t = min(ATTN_TILE, s)
    hp = ATTN_HEADS_PER_STEP

    w_in_bf, w_pool_bf, w_mem_kv_bf = w_in.astype(BF16), w_pool.astype(BF16), w_mem_kv.astype(BF16)
    w_branch_bf, w_out_bf = w_branch.astype(BF16), w_out.astype(BF16)
    lamv = jnp.stack([lam_q1, lam_k1, lam_q2, lam_k2], axis=1)
    n_gate_blocks = (w_in.shape[2] - N_SLICES * BRANCH_WIDTH) // BRANCH_WIDTH
    rest_blocks = REST_SLICES + tuple(range(N_SLICES, N_SLICES + n_gate_blocks))
    mkv_blocks = tuple(range(w_mem_kv.shape[2] // BRANCH_WIDTH))

    bias = bias_tiles(rel_bias, t)
    x2d = x.reshape(b * s, d)
    mem2d = mem.reshape(b * n_mem, d)
    h = rmsnorm_bf16(x2d, norm_g, 0)
    for l in range(depth):
        lam_init = 0.8 - 0.6 * math.exp(-0.3 * l)
        qkv = matmul_cols(h, w_in_bf, l, QKV_SLICES, BF16).reshape(b, s, -1)
        rest2d = matmul_cols(h, w_in_bf, l, rest_blocks, F32)
        rest = rest2d.reshape(b, s, -1)

        br_da, br_sb = attention_branches(qkv, rest, bias, lamv, da_norm_g, l, lam_init, t, hp)
        br_pool = pool_branch(rest, w_pool_bf, pool_scale, l)
        mkv = matmul_cols(rmsnorm_bf16(mem2d, mem_norm_g, l), w_mem_kv_bf, l, mkv_blocks, F32)
        br_mem = memory_branch(qkv, rest, mkv.reshape(b, n_mem, 2 * BRANCH_WIDTH))

        branches = [br.reshape(b * s, BRANCH_WIDTH) for br in (br_da, br_sb, br_pool, br_mem)]
        merged = gated_merge(branches, rest2d, gate_b, w_branch_bf, l)
        if l == depth - 1:
            (out2d,) = matmul_residual(merged, w_out_bf, l, x2d, final_g, 0, final_norm=True)
        else:
            x2d, h = matmul_residual(merged, w_out_bf, l, x2d, norm_g, l + 1, final_norm=False)
    return out2d.reshape(b, s, d)
```

```python
import functools
import math

import jax
import jax.numpy as jnp
from jax import lax
from jax.experimental import pallas as pl
from jax.experimental.pallas import tpu as pltpu

F32 = jnp.float32
BF16 = jnp.bfloat16

BRANCH_WIDTH = 1024
N_BRANCH = 4
N_SLICES = 12
DA_HEADS = 8
DA_QK_DIM = 64
DA_V_DIM = 2 * DA_QK_DIM
SB_HEADS = 8
SB_HEAD_DIM = BRANCH_WIDTH // SB_HEADS
POOL_WINDOWS = (2, 4, 8, 16)
POOL_GROUP = BRANCH_WIDTH // len(POOL_WINDOWS)
MEM_HEADS = 4
MEM_HEAD_DIM = BRANCH_WIDTH // MEM_HEADS
REL_BUCKETS = 32
REL_MAX_DIST = 128
EPS = 1e-6

SL_DA_Q, SL_DA_K, SL_DA_V, SL_DA_Z = 0, 1, 2, 3
SL_SB_Q, SL_SB_K, SL_SB_V, SL_SB_Z = 4, 5, 6, 7
SL_POOL_U, SL_POOL_Z, SL_MEM_Q, SL_MEM_Z = 8, 9, 10, 11
QKV_SLICES = (SL_DA_Q, SL_DA_K, SL_DA_V, SL_SB_Q, SL_SB_K, SL_SB_V, SL_MEM_Q)
REST_SLICES = (SL_DA_Z, SL_SB_Z, SL_POOL_U, SL_POOL_Z, SL_MEM_Z)
QKV_DA_Q, QKV_DA_K, QKV_DA_V, QKV_SB_Q, QKV_SB_K, QKV_SB_V, QKV_MEM_Q = range(len(QKV_SLICES))
REST_DA_Z, REST_SB_Z, REST_POOL_U, REST_POOL_Z, REST_MEM_Z, REST_GATES = range(len(REST_SLICES) + 1)

V7X_VMEM_LIMIT_BYTES = 56 * 1024 * 1024
LANES = 128
POOL_HALO = 16
NEG_BIG = -1e30
ATTN_TILE = 512
ATTN_HEADS_PER_STEP = 2
SB_CUMSUM_BLOCK = 256
SB_DEAD_LOG2 = 160.0
LOG2E = 1.4426950408889634
LN2 = 0.6931471805599453


def _params(*sem):
    return pltpu.CompilerParams(dimension_semantics=sem, vmem_limit_bytes=V7X_VMEM_LIMIT_BYTES)


def _tile(n, t):
    t = min(t, n)
    assert n % t == 0, (n, t)
    return t


def _silu(z):
    return z * (1.0 / (1.0 + jnp.exp(-z)))


def _dot_nt(a, b):
    return lax.dot_general(a, b, (((1,), (1,)), ((), ())), preferred_element_type=F32)


def _dot(a, b):
    return jnp.dot(a, b, preferred_element_type=F32)


def _lane_tile(x, n):
    return x if n == 1 else jnp.concatenate([x] * n, axis=1)


def _neg_abs(x):
    u = lax.bitcast_convert_type(x, jnp.uint32) | jnp.uint32(0x80000000)
    return lax.bitcast_convert_type(u, F32)


def _rmsnorm_kernel(x_ref, g_ref, o_ref):
    x = x_ref[...]
    y = x * lax.rsqrt(jnp.mean(x * x, axis=-1, keepdims=True) + EPS)
    o_ref[...] = (y * g_ref[...]).astype(o_ref.dtype)


def rmsnorm_bf16(x2d, gains, layer, tr=512):
    m, d = x2d.shape
    tr = _tile(m, tr)
    return pl.pallas_call(
        _rmsnorm_kernel,
        grid=(m // tr,),
        in_specs=[
            pl.BlockSpec((tr, d), lambda i: (i, 0)),
            pl.BlockSpec((None, 1, d), lambda i: (layer, 0, 0)),
        ],
        out_specs=pl.BlockSpec((tr, d), lambda i: (i, 0)),
        out_shape=jax.ShapeDtypeStruct((m, d), BF16),
        compiler_params=_params("parallel"),
        name="rmsnorm",
    )(x2d, gains.reshape(-1, 1, d))


def _matmul_kernel(h_ref, w_ref, o_ref):
    o_ref[...] = _dot(h_ref[...], w_ref[...]).astype(o_ref.dtype)


def _static_lookup(table, j):
    r = jnp.int32(table[0])
    for idx in range(1, len(table)):
        r = jnp.where(j == idx, jnp.int32(table[idx]), r)
    return r


def matmul_cols(h, w, layer, col_blocks, out_dtype, tm=2048, tn=BRANCH_WIDTH):
    m, d = h.shape
    tm = _tile(m, tm)
    assert w.shape[2] % tn == 0
    n_out = len(col_blocks)
    return pl.pallas_call(
        _matmul_kernel,
        grid=(m // tm, n_out),
        in_specs=[
            pl.BlockSpec((tm, d), lambda i, j: (i, 0)),
            pl.BlockSpec((None, d, tn), lambda i, j: (layer, 0, _static_lookup(col_blocks, j))),
        ],
        out_specs=pl.BlockSpec((tm, tn), lambda i, j: (i, j)),
        out_shape=jax.ShapeDtypeStruct((m, n_out * tn), out_dtype),
        compiler_params=_params("parallel", "parallel"),
        name="in_proj",
    )(h, w)


def _matmul_residual_kernel(a_ref, w_ref, r_ref, g_ref, *out_refs, final_norm):
    y = r_ref[...] + _dot(a_ref[...], w_ref[...])
    normed = (y * lax.rsqrt(jnp.mean(y * y, axis=-1, keepdims=True) + EPS)) * g_ref[...]
    if final_norm:
        out_refs[0][...] = normed
    else:
        out_refs[0][...] = y
        out_refs[1][...] = normed.astype(BF16)


def matmul_residual(a_bf16, w_bf16, layer, res, gains, gain_index, final_norm, tm=512):
    m, k = a_bf16.shape
    n = w_bf16.shape[2]
    tm = _tile(m, tm)
    row_spec = pl.BlockSpec((tm, n), lambda i: (i, 0))
    out_shapes = [jax.ShapeDtypeStruct((m, n), F32)]
    if not final_norm:
        out_shapes.append(jax.ShapeDtypeStruct((m, n), BF16))
    return pl.pallas_call(
        functools.partial(_matmul_residual_kernel, final_norm=final_norm),
        grid=(m // tm,),
        in_specs=[
            pl.BlockSpec((tm, k), lambda i: (i, 0)),
            pl.BlockSpec((None, k, n), lambda i: (layer, 0, 0)),
            row_spec,
            pl.BlockSpec((None, 1, n), lambda i: (gain_index, 0, 0)),
        ],
        out_specs=[row_spec] * len(out_shapes),
        out_shape=out_shapes,
        compiler_params=_params("parallel"),
        name="out_proj_residual",
    )(a_bf16, w_bf16, res, gains.reshape(-1, 1, n))


def _bias_tiles_kernel(rb_ref, o_ref, *, t):
    h = pl.program_id(0)
    blk = REL_MAX_DIST
    nb = t // blk
    qi = lax.broadcasted_iota(jnp.int32, (blk, blk), 0)
    ki = lax.broadcasted_iota(jnp.int32, (blk, blk), 1)
    max_exact = REL_BUCKETS // 2
    far = rb_ref[REL_BUCKETS - 1, h]

    def band(offset):
        n = jnp.maximum(qi - ki + offset, 0)
        nf = jnp.maximum(n, 1).astype(F32)
        large = max_exact + (jnp.log(nf / max_exact) / math.log(REL_MAX_DIST / max_exact)
                             * (REL_BUCKETS - max_exact)).astype(jnp.int32)
        large = jnp.minimum(large, REL_BUCKETS - 1)
        bucket = jnp.where(n < max_exact, n, large)
        val = jnp.zeros((blk, blk), F32)
        for b in range(REL_BUCKETS - 1):
            val = jnp.where(bucket == b, rb_ref[b, h] - far, val)
        return val

    on_diag = jnp.where(qi >= ki, band(0), NEG_BIG)
    below = band(blk)
    zeros = jnp.zeros((blk, blk), F32)
    masked = jnp.full((blk, blk), NEG_BIG, F32)
    for bi in range(nb):
        for bj in range(nb):
            rows, cols = slice(bi * blk, (bi + 1) * blk), slice(bj * blk, (bj + 1) * blk)
            o_ref[0, 0, rows, cols] = (on_diag if bj == bi else below if bj == bi - 1
                                       else masked if bj > bi else zeros)
            o_ref[0, 1, rows, cols] = below if (bi == 0 and bj == nb - 1) else zeros


def bias_tiles(rel_bias, t):
    assert t % REL_MAX_DIST == 0
    heads = rel_bias.shape[1]
    return pl.pallas_call(
        functools.partial(_bias_tiles_kernel, t=t),
        grid=(heads,),
        in_specs=[pl.BlockSpec(memory_space=pltpu.SMEM)],
        out_specs=pl.BlockSpec((1, 2, t, t), lambda h: (h, 0, 0, 0)),
        out_shape=jax.ShapeDtypeStruct((heads, 2, t, t), F32),
        compiler_params=_params("parallel"),
        name="t5_bias_tiles",
    )(rel_bias)


def _attn_kernel(lamv_ref, dq_ref, dk_ref, dv_ref, dz_ref, bias_ref, g_ref,
                 sq_ref, sk_ref, sv_ref, sz_ref, do_ref, so_ref,
                 dkt_scr, dv_scr, dq_scr, m_scr, dacc_scr, skt_scr, sq_scr, c_scr, sacc_scr,
                 *, t, hp, lam_init):
    qi = pl.program_id(2)
    nt = dkt_scr.shape[1]
    hd = DA_V_DIM
    cb = min(SB_CUMSUM_BLOCK, t)

    @pl.when(qi == 0)
    def _():
        for h in range(hp):
            for jj in range(nt):
                dkt_scr[h, jj] = dk_ref[0, jj * t:(jj + 1) * t, h * hd:(h + 1) * hd].astype(F32).T.astype(BF16)
            for jb in range(skt_scr.shape[1]):
                skt_scr[h, jb] = sk_ref[0, jb * cb:(jb + 1) * cb, h * hd:(h + 1) * hd].astype(F32).T.astype(BF16)
            dv_scr[h, :, 0:hd] = dv_ref[0, :, h * hd:(h + 1) * hd]
            dv_scr[h, :, hd:2 * hd] = jnp.ones((dv_scr.shape[1], hd), BF16)

    lane = lax.broadcasted_iota(jnp.int32, (t, hd), 1)
    for h in range(hp):
        q = dq_ref[0, :, h * hd:(h + 1) * hd] * (DA_QK_DIM ** -0.5)
        dq_scr[h, 0] = jnp.where(lane < DA_QK_DIM, q, 0.0).astype(BF16)
        dq_scr[h, 1] = jnp.where(lane >= DA_QK_DIM, q, 0.0).astype(BF16)
        sq_scr[h] = (sq_ref[0, :, h * hd:(h + 1) * hd].astype(F32) * (SB_HEAD_DIM ** -0.5 * LOG2E)).astype(BF16)
    m_scr[...] = jnp.full(m_scr.shape, -jnp.inf, F32)
    dacc_scr[...] = jnp.zeros(dacc_scr.shape, F32)
    c_scr[...] = jnp.zeros(c_scr.shape, F32)
    sacc_scr[...] = jnp.zeros(sacc_scr.shape, F32)

    def da_tile(j, bias_sel):
        off = pl.multiple_of(j * t, t)
        for h in range(hp):
            kt = dkt_scr[h, j]
            v = dv_scr[h, pl.ds(off, t), :]
            for c in range(2):
                s = _dot(dq_scr[h, c], kt)
                if bias_sel is not None:
                    s = s + bias_ref[h, bias_sel]
                m_prev = m_scr[h, c]
                m_new = jnp.maximum(m_prev, jnp.max(s, axis=-1, keepdims=True))
                alpha = jnp.exp(m_prev - m_new)
                p = jnp.exp(s - _lane_tile(m_new, t // LANES))
                dacc_scr[h, c] = _lane_tile(alpha, 2) * dacc_scr[h, c] + _dot(p.astype(BF16), v)
                m_scr[h, c] = m_new

    def sb_block(jb, mask_off):
        off = pl.multiple_of(jb * cb, cb)
        r2 = lax.broadcasted_iota(jnp.int32, (cb, cb), 0)
        c2 = lax.broadcasted_iota(jnp.int32, (cb, cb), 1)
        upper = jnp.where(r2 > c2, 1.0, 0.0).astype(BF16)
        if mask_off is not None:
            row = lax.broadcasted_iota(jnp.int32, (t, cb), 0)
            col = lax.broadcasted_iota(jnp.int32, (t, cb), 1)
            causal = col + mask_off < row
        for h in range(hp):
            v = sv_ref[0, pl.ds(off, cb), h * hd:(h + 1) * hd]
            z2 = _dot(sq_scr[h], skt_scr[h, jb])
            w = jnp.maximum(z2, 0.0) + jnp.log(1.0 + jnp.exp2(_neg_abs(z2))) * (1.0 / LN2)
            log2_beta = z2 - w
            if mask_off is not None:
                w = jnp.where(causal, w, 0.0)
            c = c_scr[h]
            cs = _dot(w.astype(BF16), upper)
            a = jnp.exp2((log2_beta - _lane_tile(c, cb // LANES)) - cs)
            if mask_off is not None:
                a = jnp.where(causal, a, 0.0)
            c_scr[h] = c + jnp.broadcast_to(cs[:, 0:1] + w[:, 0:1], c.shape)
            sacc_scr[h] += _dot(a.astype(BF16), v)

    nblk = t // cb

    def sb_diag_blocks():
        for bb in reversed(range(nblk)):
            sb_block(qi * nblk + bb, bb * cb)

    n_far = jnp.maximum(qi - 1, 0)

    def far_body(jp, carry):
        da_tile(2 * jp, None)
        da_tile(2 * jp + 1, None)
        return carry

    lax.fori_loop(0, n_far // 2, far_body, 0)

    @pl.when(n_far % 2 == 1)
    def _():
        da_tile(n_far - 1, None)

    @pl.when(qi == 0)
    def _():
        da_tile(0, 0)
        sb_diag_blocks()

    @pl.when(qi >= 1)
    def _():
        da_tile(qi - 1, 1)
        sb_diag_blocks()
        da_tile(qi, 0)
        sb_block(qi * nblk - 1, None)

    def min_decay():
        c = c_scr[0]
        for h in range(1, hp):
            c = jnp.minimum(c, c_scr[h])
        return jnp.min(c)

    def alive(carry):
        j, cmin = carry
        return (j >= 0) & (cmin < SB_DEAD_LOG2)

    def sb_body(carry):
        j, _ = carry
        sb_block(j, None)
        return j - 1, min_decay()

    lax.while_loop(alive, sb_body, (qi * nblk - 2, min_decay()))

    lv = lamv_ref[...]
    s1 = jnp.sum(lv[0:1] * lv[1:2], axis=-1, keepdims=True)
    s2 = jnp.sum(lv[2:3] * lv[3:4], axis=-1, keepdims=True)
    lam = jnp.exp(s1) - jnp.exp(s2) + lam_init
    for h in range(hp):
        cols = slice(h * hd, (h + 1) * hd)
        a0 = dacc_scr[h, 0]
        a1 = dacc_scr[h, 1]
        o = a0[:, 0:hd] / a0[:, hd:2 * hd] - lam * (a1[:, 0:hd] / a1[:, hd:2 * hd])
        y = o * lax.rsqrt(jnp.mean(o * o, axis=-1, keepdims=True) + EPS)
        y = (y * g_ref[:, cols]) * (1.0 - lam_init)
        do_ref[0, :, cols] = (y * _silu(dz_ref[0, :, cols])).astype(do_ref.dtype)
        so_ref[0, :, cols] = (sacc_scr[h] * _silu(sz_ref[0, :, cols])).astype(so_ref.dtype)


def attention_branches(qkv, rest, bias, lamv, da_norm_g, layer, lam_init, t, hp):
    b, s, _ = qkv.shape
    t = _tile(s, t)
    hd = DA_V_DIM
    w = hp * hd
    cpb = BRANCH_WIDTH // w
    cb = min(SB_CUMSUM_BLOCK, t)
    q_spec = lambda sl: pl.BlockSpec((1, t, w), lambda bi, h, qi: (bi, qi, sl * cpb + h))
    kv_spec = lambda sl: pl.BlockSpec((1, s, w), lambda bi, h, qi: (bi, 0, sl * cpb + h))
    out_spec = pl.BlockSpec((1, t, w), lambda bi, h, qi: (bi, qi, h))
    return pl.pallas_call(
        functools.partial(_attn_kernel, t=t, hp=hp, lam_init=lam_init),
        grid=(b, DA_HEADS // hp, s // t),
        in_specs=[
            pl.BlockSpec((None, 4, DA_QK_DIM), lambda bi, h, qi: (layer, 0, 0)),
            q_spec(QKV_DA_Q), kv_spec(QKV_DA_K), kv_spec(QKV_DA_V), q_spec(REST_DA_Z),
            pl.BlockSpec((hp, 2, t, t), lambda bi, h, qi: (h, 0, 0, 0)),
            pl.BlockSpec((None, 1, w), lambda bi, h, qi: (layer, 0, h)),
            q_spec(QKV_SB_Q), kv_spec(QKV_SB_K), kv_spec(QKV_SB_V), q_spec(REST_SB_Z),
        ],
        out_specs=[out_spec, out_spec],
        out_shape=[jax.ShapeDtypeStruct((b, s, BRANCH_WIDTH), BF16)] * 2,
        scratch_shapes=[
            pltpu.VMEM((hp, s // t, hd, t), BF16),
            pltpu.VMEM((hp, s, 2 * hd), BF16),
            pltpu.VMEM((hp, 2, t, hd), BF16),
            pltpu.VMEM((hp, 2, t, LANES), F32),
            pltpu.VMEM((hp, 2, t, 2 * hd), F32),
            pltpu.VMEM((hp, s // cb, hd, cb), BF16),
            pltpu.VMEM((hp, t, hd), BF16),
            pltpu.VMEM((hp, t, LANES), F32),
            pltpu.VMEM((hp, t, hd), F32),
        ],
        compiler_params=_params("parallel", "parallel", "arbitrary"),
        name="causal_attention",
    )(lamv, qkv, qkv, qkv, rest, bias, da_norm_g.reshape(-1, 1, BRANCH_WIDTH), qkv, qkv, qkv, rest)


def _pool_kernel(u_ref, z_ref, w_ref, sc_ref, o_ref, ext_scr, *, tt):
    ti = pl.program_id(1)

    @pl.when(ti == 0)
    def _():
        ext_scr[0:POOL_HALO] = jnp.zeros((POOL_HALO, BRANCH_WIDTH), F32)

    ext_scr[POOL_HALO:POOL_HALO + tt] = u_ref[0]
    pos = ti * tt + lax.broadcasted_iota(jnp.int32, (tt, 1), 0)
    for g, w in enumerate(POOL_WINDOWS):
        cols = slice(g * POOL_GROUP, (g + 1) * POOL_GROUP)
        u = ext_scr[POOL_HALO:POOL_HALO + tt, cols]
        win = u
        for j in range(1, w):
            win = win + ext_scr[POOL_HALO - j:POOL_HALO - j + tt, cols]
        count = jnp.minimum(pos + 1, w).astype(F32)
        pooled = win / count - u
        mixed = _dot(pooled.astype(BF16), w_ref[g]) * sc_ref[:, cols]
        o_ref[0, :, cols] = (mixed * _silu(z_ref[0, :, cols])).astype(o_ref.dtype)
    ext_scr[0:POOL_HALO] = ext_scr[tt:tt + POOL_HALO]


def pool_branch(rest, w_pool_bf16, pool_scale, layer, tt=1024):
    b, s, _ = rest.shape
    tt = _tile(s, tt)
    assert tt >= POOL_HALO and max(POOL_WINDOWS) <= POOL_HALO
    return pl.pallas_call(
        functools.partial(_pool_kernel, tt=tt),
        grid=(b, s // tt),
        in_specs=[
            pl.BlockSpec((1, tt, BRANCH_WIDTH), lambda bi, ti: (bi, ti, REST_POOL_U)),
            pl.BlockSpec((1, tt, BRANCH_WIDTH), lambda bi, ti: (bi, ti, REST_POOL_Z)),
            pl.BlockSpec((None,) + w_pool_bf16.shape[1:], lambda bi, ti: (layer, 0, 0, 0)),
            pl.BlockSpec((None, 1, BRANCH_WIDTH), lambda bi, ti: (layer, 0, 0)),
        ],
        out_specs=pl.BlockSpec((1, tt, BRANCH_WIDTH), lambda bi, ti: (bi, ti, 0)),
        out_shape=jax.ShapeDtypeStruct((b, s, BRANCH_WIDTH), BF16),
        scratch_shapes=[pltpu.VMEM((tt + POOL_HALO, BRANCH_WIDTH), F32)],
        compiler_params=_params("parallel", "arbitrary"),
        name="multiscale_pool",
    )(rest, rest, w_pool_bf16, pool_scale.reshape(-1, 1, BRANCH_WIDTH))


def _mem_attn_kernel(q_ref, z_ref, mk_ref, mv_ref, o_ref):
    hd = MEM_HEAD_DIM
    for h in range(MEM_HEADS):
        cols = slice(h * hd, (h + 1) * hd)
        q = q_ref[0, :, cols] * (MEM_HEAD_DIM ** -0.5)
        s = _dot_nt(q, mk_ref[0, :, cols].astype(BF16))
        p = jnp.exp(s - jnp.max(s, axis=-1, keepdims=True))
        l = jnp.sum(p, axis=-1, keepdims=True)
        o = _dot(p.astype(BF16), mv_ref[0, :, cols].astype(BF16)) / l
        o_ref[0, :, cols] = (o * _silu(z_ref[0, :, cols])).astype(o_ref.dtype)


def memory_branch(qkv, rest, mkv, tq=1024):
    b, s, _ = qkv.shape
    m = mkv.shape[1]
    tq = _tile(s, tq)
    w = BRANCH_WIDTH
    return pl.pallas_call(
        _mem_attn_kernel,
        grid=(b, s // tq),
        in_specs=[
            pl.BlockSpec((1, tq, w), lambda bi, qi: (bi, qi, QKV_MEM_Q)),
            pl.BlockSpec((1, tq, w), lambda bi, qi: (bi, qi, REST_MEM_Z)),
            pl.BlockSpec((1, m, w), lambda bi, qi: (bi, 0, 0)),
            pl.BlockSpec((1, m, w), lambda bi, qi: (bi, 0, 1)),
        ],
        out_specs=pl.BlockSpec((1, tq, w), lambda bi, qi: (bi, qi, 0)),
        out_shape=jax.ShapeDtypeStruct((b, s, BRANCH_WIDTH), BF16),
        compiler_params=_params("parallel", "parallel"),
        name="memory_attention",
    )(qkv, rest, mkv, mkv)


def _merge_kernel(b0_ref, b1_ref, b2_ref, b3_ref, g0_ref, g1_ref, g2_ref, g3_ref, gb_ref, w_ref, o_ref):
    branches = (b0_ref, b1_ref, b2_ref, b3_ref)
    gates = (g0_ref, g1_ref, g2_ref, g3_ref)
    merged = None
    for n in range(N_BRANCH):
        gate = 1.0 / (1.0 + jnp.exp(-(gates[n][...] + gb_ref[n:n + 1, :])))
        term = gate * _dot(branches[n][...], w_ref[n])
        merged = term if merged is None else merged + term
    o_ref[...] = merged.astype(o_ref.dtype)


def gated_merge(branches, rest2d, gate_b, w_branch_bf16, layer, tm=512, tn=1024):
    m = rest2d.shape[0]
    d = w_branch_bf16.shape[3]
    tm, tn = _tile(m, tm), _tile(d, tn)
    gate_col0 = REST_GATES * BRANCH_WIDTH // tn
    per_gate = d // tn
    br_spec = pl.BlockSpec((tm, BRANCH_WIDTH), lambda j, i: (i, 0))

    def gate_spec(n):
        return pl.BlockSpec((tm, tn), lambda j, i: (i, gate_col0 + n * per_gate + j))

    return pl.pallas_call(
        _merge_kernel,
        grid=(d // tn, m // tm),
        in_specs=[br_spec] * N_BRANCH + [gate_spec(n) for n in range(N_BRANCH)] + [
            pl.BlockSpec((None, N_BRANCH, tn), lambda j, i: (layer, 0, j)),
            pl.BlockSpec((None, N_BRANCH, BRANCH_WIDTH, tn), lambda j, i: (layer, 0, 0, j)),
        ],
        out_specs=pl.BlockSpec((tm, tn), lambda j, i: (i, j)),
        out_shape=jax.ShapeDtypeStruct((m, d), BF16),
        compiler_params=_params("parallel", "parallel"),
        name="gated_merge",
    )(*branches, rest2d, rest2d, rest2d, rest2d, gate_b, w_branch_bf16)


def kernel(x, mem, rel_bias, norm_g, w_in, gate_b, lam_q1, lam_k1, lam_q2, lam_k2, da_norm_g, w_pool,
           pool_scale, mem_norm_g, w_mem_kv, w_branch, w_out, final_g):
    b, s, d = x.shape
    depth = norm_g.shape[0]
    n_mem = mem.shape[1]
    t = min(ATTN_TILE, s)
    hp = ATTN_HEADS_PER_STEP

    w_in_bf, w_pool_bf, w_mem_kv_bf = w_in.astype(BF16), w_pool.astype(BF16), w_mem_kv.astype(BF16)
    w_branch_bf, w_out_bf = w_branch.astype(BF16), w_out.astype(BF16)
    lamv = jnp.stack([lam_q1, lam_k1, lam_q2, lam_k2], axis=1)
    n_gate_blocks = (w_in.shape[2] - N_SLICES * BRANCH_WIDTH) // BRANCH_WIDTH
    rest_blocks = REST_SLICES + tuple(range(N_SLICES, N_SLICES + n_gate_blocks))
    mkv_blocks = tuple(range(w_mem_kv.shape[2] // BRANCH_WIDTH))

    bias = bias_tiles(rel_bias, t)
    x2d = x.reshape(b * s, d)
    mem2d = mem.reshape(b * n_mem, d)
    h = rmsnorm_bf16(x2d, norm_g, 0)
    for l in range(depth):
        lam_init = 0.8 - 0.6 * math.exp(-0.3 * l)
        qkv = matmul_cols(h, w_in_bf, l, QKV_SLICES, BF16).reshape(b, s, -1)
        rest2d = matmul_cols(h, w_in_bf, l, rest_blocks, F32)
        rest = rest2d.reshape(b, s, -1)

        br_da, br_sb = attention_branches(qkv, rest, bias, lamv, da_norm_g, l, lam_init, t, hp)
        br_pool = pool_branch(rest, w_pool_bf, pool_scale, l)
        mkv = matmul_cols(rmsnorm_bf16(mem2d, mem_norm_g, l), w_mem_kv_bf, l, mkv_blocks, F32)
        br_mem = memory_branch(qkv, rest, mkv.reshape(b, n_mem, 2 * BRANCH_WIDTH))

        branches = [br.reshape(b * s, BRANCH_WIDTH) for br in (br_da, br_sb, br_pool, br_mem)]
        merged = gated_merge(branches, rest2d, gate_b, w_branch_bf, l)
        if l == depth - 1:
            (out2d,) = matmul_residual(merged, w_out_bf, l, x2d, final_g, 0, final_norm=True)
        else:
            x2d, h = matmul_residual(merged, w_out_bf, l, x2d, norm_g, l + 1, final_norm=False)
    return out2d.reshape(b, s, d)
```

```python
import functools
import math

import jax
import jax.numpy as jnp
from jax import lax
from jax.experimental import pallas as pl
from jax.experimental.pallas import tpu as pltpu

F32 = jnp.float32
BF16 = jnp.bfloat16

BRANCH_WIDTH = 1024
N_BRANCH = 4
N_SLICES = 12
DA_HEADS = 8
DA_QK_DIM = 64
DA_V_DIM = 2 * DA_QK_DIM
SB_HEADS = 8
SB_HEAD_DIM = BRANCH_WIDTH // SB_HEADS
POOL_WINDOWS = (2, 4, 8, 16)
POOL_GROUP = BRANCH_WIDTH // len(POOL_WINDOWS)
MEM_HEADS = 4
MEM_HEAD_DIM = BRANCH_WIDTH // MEM_HEADS
REL_BUCKETS = 32
REL_MAX_DIST = 128
EPS = 1e-6

SL_DA_Q, SL_DA_K, SL_DA_V, SL_DA_Z = 0, 1, 2, 3
SL_SB_Q, SL_SB_K, SL_SB_V, SL_SB_Z = 4, 5, 6, 7
SL_POOL_U, SL_POOL_Z, SL_MEM_Q, SL_MEM_Z = 8, 9, 10, 11
QKV_SLICES = (SL_DA_Q, SL_DA_K, SL_DA_V, SL_SB_Q, SL_SB_K, SL_SB_V, SL_MEM_Q)
REST_SLICES = (SL_DA_Z, SL_SB_Z, SL_POOL_U, SL_POOL_Z, SL_MEM_Z)
QKV_DA_Q, QKV_DA_K, QKV_DA_V, QKV_SB_Q, QKV_SB_K, QKV_SB_V, QKV_MEM_Q = range(len(QKV_SLICES))
REST_DA_Z, REST_SB_Z, REST_POOL_U, REST_POOL_Z, REST_MEM_Z, REST_GATES = range(len(REST_SLICES) + 1)

V7X_VMEM_LIMIT_BYTES = 56 * 1024 * 1024
LANES = 128
POOL_HALO = 16
NEG_BIG = -1e30
ATTN_TILE = 512
ATTN_HEADS_PER_STEP = 2
SB_CUMSUM_BLOCK = 256
SB_DEAD_LOG2 = 160.0
LOG2E = 1.4426950408889634
LN2 = 0.6931471805599453


def _params(*sem):
    return pltpu.CompilerParams(dimension_semantics=sem, vmem_limit_bytes=V7X_VMEM_LIMIT_BYTES)


def _tile(n, t):
    t = min(t, n)
    assert n % t == 0, (n, t)
    return t


def _silu(z):
    return z * (1.0 / (1.0 + jnp.exp(-z)))


def _dot_nt(a, b):
    return lax.dot_general(a, b, (((1,), (1,)), ((), ())), preferred_element_type=F32)


def _dot(a, b):
    return jnp.dot(a, b, preferred_element_type=F32)


def _lane_tile(x, n):
    return x if n == 1 else jnp.concatenate([x] * n, axis=1)


def _neg_abs(x):
    u = lax.bitcast_convert_type(x, jnp.uint32) | jnp.uint32(0x80000000)
    return lax.bitcast_convert_type(u, F32)


def _rmsnorm_kernel(x_ref, g_ref, o_ref):
    x = x_ref[...]
    y = x * lax.rsqrt(jnp.mean(x * x, axis=-1, keepdims=True) + EPS)
    o_ref[...] = (y * g_ref[...]).astype(o_ref.dtype)


def rmsnorm_bf16(x2d, gains, layer, tr=512):
    m, d = x2d.shape
    tr = _tile(m, tr)
    return pl.pallas_call(
        _rmsnorm_kernel,
        grid=(m // tr,),
        in_specs=[
            pl.BlockSpec((tr, d), lambda i: (i, 0)),
            pl.BlockSpec((None, 1, d), lambda i: (layer, 0, 0)),
        ],
        out_specs=pl.BlockSpec((tr, d), lambda i: (i, 0)),
        out_shape=jax.ShapeDtypeStruct((m, d), BF16),
        compiler_params=_params("parallel"),
        name="rmsnorm",
    )(x2d, gains.reshape(-1, 1, d))


def _matmul_kernel(h_ref, w_ref, o_ref):
    o_ref[...] = _dot(h_ref[...], w_ref[...]).astype(o_ref.dtype)


def _static_lookup(table, j):
    r = jnp.int32(table[0])
    for idx in range(1, len(table)):
        r = jnp.where(j == idx, jnp.int32(table[idx]), r)
    return r


def matmul_cols(h, w, layer, col_blocks, out_dtype, tm=2048, tn=BRANCH_WIDTH):
    m, d = h.shape
    tm = _tile(m, tm)
    assert w.shape[2] % tn == 0
    n_out = len(col_blocks)
    return pl.pallas_call(
        _matmul_kernel,
        grid=(m // tm, n_out),
        in_specs=[
            pl.BlockSpec((tm, d), lambda i, j: (i, 0)),
            pl.BlockSpec((None, d, tn), lambda i, j: (layer, 0, _static_lookup(col_blocks, j))),
        ],
        out_specs=pl.BlockSpec((tm, tn), lambda i, j: (i, j)),
        out_shape=jax.ShapeDtypeStruct((m, n_out * tn), out_dtype),
        compiler_params=_params("parallel", "parallel"),
        name="in_proj",
    )(h, w)


def _matmul_residual_kernel(a_ref, w_ref, r_ref, g_ref, *out_refs, final_norm):
    y = r_ref[...] + _dot(a_ref[...], w_ref[...])
    normed = (y * lax.rsqrt(jnp.mean(y * y, axis=-1, keepdims=True) + EPS)) * g_ref[...]
    if final_norm:
        out_refs[0][...] = normed
    else:
        out_refs[0][...] = y
        out_refs[1][...] = normed.astype(BF16)


def matmul_residual(a_bf16, w_bf16, layer, res, gains, gain_index, final_norm, tm=512):
    m, k = a_bf16.shape
    n = w_bf16.shape[2]
    tm = _tile(m, tm)
    row_spec = pl.BlockSpec((tm, n), lambda i: (i, 0))
    out_shapes = [jax.ShapeDtypeStruct((m, n), F32)]
    if not final_norm:
        out_shapes.append(jax.ShapeDtypeStruct((m, n), BF16))
    return pl.pallas_call(
        functools.partial(_matmul_residual_kernel, final_norm=final_norm),
        grid=(m // tm,),
        in_specs=[
            pl.BlockSpec((tm, k), lambda i: (i, 0)),
            pl.BlockSpec((None, k, n), lambda i: (layer, 0, 0)),
            row_spec,
            pl.BlockSpec((None, 1, n), lambda i: (gain_index, 0, 0)),
        ],
        out_specs=[row_spec] * len(out_shapes),
        out_shape=out_shapes,
        compiler_params=_params("parallel"),
        name="out_proj_residual",
    )(a_bf16, w_bf16, res, gains.reshape(-1, 1, n))


def _bias_tiles_kernel(rb_ref, o_ref, *, t):
    h = pl.program_id(0)
    blk = REL_MAX_DIST
    nb = t // blk
    qi = lax.broadcasted_iota(jnp.int32, (blk, blk), 0)
    ki = lax.broadcasted_iota(jnp.int32, (blk, blk), 1)
    max_exact = REL_BUCKETS // 2
    far = rb_ref[REL_BUCKETS - 1, h]

    def band(offset):
        n = jnp.maximum(qi - ki + offset, 0)
        nf = jnp.maximum(n, 1).astype(F32)
        large = max_exact + (jnp.log(nf / max_exact) / math.log(REL_MAX_DIST / max_exact)
                             * (REL_BUCKETS - max_exact)).astype(jnp.int32)
        large = jnp.minimum(large, REL_BUCKETS - 1)
        bucket = jnp.where(n < max_exact, n, large)
        val = jnp.zeros((blk, blk), F32)
        for b in range(REL_BUCKETS - 1):
            val = jnp.where(bucket == b, rb_ref[b, h] - far, val)
        return val

    on_diag = jnp.where(qi >= ki, band(0), NEG_BIG)
    below = band(blk)
    zeros = jnp.zeros((blk, blk), F32)
    masked = jnp.full((blk, blk), NEG_BIG, F32)
    for bi in range(nb):
        for bj in range(nb):
            rows, cols = slice(bi * blk, (bi + 1) * blk), slice(bj * blk, (bj + 1) * blk)
            o_ref[0, 0, rows, cols] = (on_diag if bj == bi else below if bj == bi - 1
                                       else masked if bj > bi else zeros)
            o_ref[0, 1, rows, cols] = below if (bi == 0 and bj == nb - 1) else zeros


def bias_tiles(rel_bias, t):
    assert t % REL_MAX_DIST == 0
    heads = rel_bias.shape[1]
    return pl.pallas_call(
        functools.partial(_bias_tiles_kernel, t=t),
        grid=(heads,),
        in_specs=[pl.BlockSpec(memory_space=pltpu.SMEM)],
        out_specs=pl.BlockSpec((1, 2, t, t), lambda h: (h, 0, 0, 0)),
        out_shape=jax.ShapeDtypeStruct((heads, 2, t, t), F32),
        compiler_params=_params("parallel"),
        name="t5_bias_tiles",
    )(rel_bias)


def _attn_kernel(lamv_ref, dq_ref, dk_ref, dv_ref, dz_ref, bias_ref, g_ref,
                 sq_ref, sk_ref, sv_ref, sz_ref, do_ref, so_ref,
                 dkt_scr, dv_scr, dq_scr, m_scr, dacc_scr, skt_scr, sq_scr, c_scr, sacc_scr,
                 *, t, hp, lam_init):
    qi = pl.program_id(2)
    nt = dkt_scr.shape[1]
    hd = DA_V_DIM
    cb = min(SB_CUMSUM_BLOCK, t)

    @pl.when(qi == 0)
    def _():
        for h in range(hp):
            for jj in range(nt):
                dkt_scr[h, jj] = dk_ref[0, jj * t:(jj + 1) * t, h * hd:(h + 1) * hd].astype(F32).T.astype(BF16)
            for jb in range(skt_scr.shape[1]):
                skt_scr[h, jb] = sk_ref[0, jb * cb:(jb + 1) * cb, h * hd:(h + 1) * hd].astype(F32).T.astype(BF16)
            dv_scr[h, :, 0:hd] = dv_ref[0, :, h * hd:(h + 1) * hd]
            dv_scr[h, :, hd:2 * hd] = jnp.ones((dv_scr.shape[1], hd), BF16)

    lane = lax.broadcasted_iota(jnp.int32, (t, hd), 1)
    for h in range(hp):
        q = dq_ref[0, :, h * hd:(h + 1) * hd] * (DA_QK_DIM ** -0.5)
        dq_scr[h, 0] = jnp.where(lane < DA_QK_DIM, q, 0.0).astype(BF16)
        dq_scr[h, 1] = jnp.where(lane >= DA_QK_DIM, q, 0.0).astype(BF16)
        sq_scr[h] = (sq_ref[0, :, h * hd:(h + 1) * hd].astype(F32) * (SB_HEAD_DIM ** -0.5 * LOG2E)).astype(BF16)
    m_scr[...] = jnp.full(m_scr.shape, -jnp.inf, F32)
    dacc_scr[...] = jnp.zeros(dacc_scr.shape, F32)
    c_scr[...] = jnp.zeros(c_scr.shape, F32)
    sacc_scr[...] = jnp.zeros(sacc_scr.shape, F32)

    def da_tile(j, bias_sel):
        off = pl.multiple_of(j * t, t)
        for h in range(hp):
            kt = dkt_scr[h, j]
            v = dv_scr[h, pl.ds(off, t), :]
            for c in range(2):
                s = _dot(dq_scr[h, c], kt)
                if bias_sel is not None:
                    s = s + bias_ref[h, bias_sel]
                m_prev = m_scr[h, c]
                m_new = jnp.maximum(m_prev, jnp.max(s, axis=-1, keepdims=True))
                alpha = jnp.exp(m_prev - m_new)
                p = jnp.exp(s - _lane_tile(m_new, t // LANES))
                dacc_scr[h, c] = _lane_tile(alpha, 2) * dacc_scr[h, c] + _dot(p.astype(BF16), v)
                m_scr[h, c] = m_new

    def sb_block(jb, mask_off):
        off = pl.multiple_of(jb * cb, cb)
        r2 = lax.broadcasted_iota(jnp.int32, (cb, cb), 0)
        c2 = lax.broadcasted_iota(jnp.int32, (cb, cb), 1)
        upper = jnp.where(r2 > c2, 1.0, 0.0).astype(BF16)
        if mask_off is not None:
            row = lax.broadcasted_iota(jnp.int32, (t, cb), 0)
            col = lax.broadcasted_iota(jnp.int32, (t, cb), 1)
            causal = col + mask_off < row
        for h in range(hp):
            v = sv_ref[0, pl.ds(off, cb), h * hd:(h + 1) * hd]
            z2 = _dot(sq_scr[h], skt_scr[h, jb])
            w = jnp.maximum(z2, 0.0) + jnp.log(1.0 + jnp.exp2(_neg_abs(z2))) * (1.0 / LN2)
            log2_beta = z2 - w
            if mask_off is not None:
                w = jnp.where(causal, w, 0.0)
            c = c_scr[h]
            cs = _dot(w.astype(BF16), upper)
            a = jnp.exp2((log2_beta - _lane_tile(c, cb // LANES)) - cs)
            if mask_off is not None:
                a = jnp.where(causal, a, 0.0)
            c_scr[h] = c + jnp.broadcast_to(cs[:, 0:1] + w[:, 0:1], c.shape)
            sacc_scr[h] += _dot(a.astype(BF16), v)

    nblk = t // cb

    def sb_diag_blocks():
        for bb in reversed(range(nblk)):
            sb_block(qi * nblk + bb, bb * cb)

    n_far = jnp.maximum(qi - 1, 0)

    def far_body(jp, carry):
        da_tile(2 * jp, None)
        da_tile(2 * jp + 1, None)
        return carry

    lax.fori_loop(0, n_far // 2, far_body, 0)

    @pl.when(n_far % 2 == 1)
    def _():
        da_tile(n_far - 1, None)

    @pl.when(qi == 0)
    def _():
        da_tile(0, 0)
        sb_diag_blocks()

    @pl.when(qi >= 1)
    def _():
        da_tile(qi - 1, 1)
        sb_diag_blocks()
        da_tile(qi, 0)
        sb_block(qi * nblk - 1, None)

    def min_decay():
        c = c_scr[0]
        for h in range(1, hp):
            c = jnp.minimum(c, c_scr[h])
        return jnp.min(c)

    def alive(carry):
        j, cmin = carry
        return (j >= 0) & (cmin < SB_DEAD_LOG2)

    def sb_body(carry):
        j, _ = carry
        sb_block(j, None)
        return j - 1, min_decay()

    lax.while_loop(alive, sb_body, (qi * nblk - 2, min_decay()))

    lv = lamv_ref[...]
    s1 = jnp.sum(lv[0:1] * lv[1:2], axis=-1, keepdims=True)
    s2 = jnp.sum(lv[2:3] * lv[3:4], axis=-1, keepdims=True)
    lam = jnp.exp(s1) - jnp.exp(s2) + lam_init
    for h in range(hp):
        cols = slice(h * hd, (h + 1) * hd)
        a0 = dacc_scr[h, 0]
        a1 = dacc_scr[h, 1]
        o = a0[:, 0:hd] / a0[:, hd:2 * hd] - lam * (a1[:, 0:hd] / a1[:, hd:2 * hd])
        y = o * lax.rsqrt(jnp.mean(o * o, axis=-1, keepdims=True) + EPS)
        y = (y * g_ref[:, cols]) * (1.0 - lam_init)
        do_ref[0, :, cols] = (y * _silu(dz_ref[0, :, cols])).astype(do_ref.dtype)
        so_ref[0, :, cols] = (sacc_scr[h] * _silu(sz_ref[0, :, cols])).astype(so_ref.dtype)


def attention_branches(qkv, rest, bias, lamv, da_norm_g, layer, lam_init, t, hp):
    b, s, _ = qkv.shape
    t = _tile(s, t)
    hd = DA_V_DIM
    w = hp * hd
    cpb = BRANCH_WIDTH // w
    cb = min(SB_CUMSUM_BLOCK, t)
    q_spec = lambda sl: pl.BlockSpec((1, t, w), lambda bi, h, qi: (bi, qi, sl * cpb + h))
    kv_spec = lambda sl: pl.BlockSpec((1, s, w), lambda bi, h, qi: (bi, 0, sl * cpb + h))
    out_spec = pl.BlockSpec((1, t, w), lambda bi, h, qi: (bi, qi, h))
    return pl.pallas_call(
        functools.partial(_attn_kernel, t=t, hp=hp, lam_init=lam_init),
        grid=(b, DA_HEADS // hp, s // t),
        in_specs=[
            pl.BlockSpec((None, 4, DA_QK_DIM), lambda bi, h, qi: (layer, 0, 0)),
            q_spec(QKV_DA_Q), kv_spec(QKV_DA_K), kv_spec(QKV_DA_V), q_spec(REST_DA_Z),
            pl.BlockSpec((hp, 2, t, t), lambda bi, h, qi: (h, 0, 0, 0)),
            pl.BlockSpec((None, 1, w), lambda bi, h, qi: (layer, 0, h)),
            q_spec(QKV_SB_Q), kv_spec(QKV_SB_K), kv_spec(QKV_SB_V), q_spec(REST_SB_Z),
        ],
        out_specs=[out_spec, out_spec],
        out_shape=[jax.ShapeDtypeStruct((b, s, BRANCH_WIDTH), BF16)] * 2,
        scratch_shapes=[
            pltpu.VMEM((hp, s // t, hd, t), BF16),
            pltpu.VMEM((hp, s, 2 * hd), BF16),
            pltpu.VMEM((hp, 2, t, hd), BF16),
            pltpu.VMEM((hp, 2, t, LANES), F32),
            pltpu.VMEM((hp, 2, t, 2 * hd), F32),
            pltpu.VMEM((hp, s // cb, hd, cb), BF16),
            pltpu.VMEM((hp, t, hd), BF16),
            pltpu.VMEM((hp, t, LANES), F32),
            pltpu.VMEM((hp, t, hd), F32),
        ],
        compiler_params=_params("parallel", "parallel", "arbitrary"),
        name="causal_attention",
    )(lamv, qkv, qkv, qkv, rest, bias, da_norm_g.reshape(-1, 1, BRANCH_WIDTH), qkv, qkv, qkv, rest)


def _pool_kernel(u_ref, z_ref, w_ref, sc_ref, o_ref, ext_scr, *, tt):
    ti = pl.program_id(1)

    @pl.when(ti == 0)
    def _():
        ext_scr[0:POOL_HALO] = jnp.zeros((POOL_HALO, BRANCH_WIDTH), F32)

    ext_scr[POOL_HALO:POOL_HALO + tt] = u_ref[0]
    pos = ti * tt + lax.broadcasted_iota(jnp.int32, (tt, 1), 0)
    for g, w in enumerate(POOL_WINDOWS):
        cols = slice(g * POOL_GROUP, (g + 1) * POOL_GROUP)
        u = ext_scr[POOL_HALO:POOL_HALO + tt, cols]
        win = u
        for j in range(1, w):
            win = win + ext_scr[POOL_HALO - j:POOL_HALO - j + tt, cols]
        count = jnp.minimum(pos + 1, w).astype(F32)
        pooled = win / count - u
        mixed = _dot(pooled.astype(BF16), w_ref[g]) * sc_ref[:, cols]
        o_ref[0, :, cols] = (mixed * _silu(z_ref[0, :, cols])).astype(o_ref.dtype)
    ext_scr[0:POOL_HALO] = ext_scr[tt:tt + POOL_HALO]


def pool_branch(rest, w_pool_bf16, pool_scale, layer, tt=2048):
    b, s, _ = rest.shape
    tt = _tile(s, tt)
    assert tt >= POOL_HALO and max(POOL_WINDOWS) <= POOL_HALO
    return pl.pallas_call(
        functools.partial(_pool_kernel, tt=tt),
        grid=(b, s // tt),
        in_specs=[
            pl.BlockSpec((1, tt, BRANCH_WIDTH), lambda bi, ti: (bi, ti, REST_POOL_U)),
            pl.BlockSpec((1, tt, BRANCH_WIDTH), lambda bi, ti: (bi, ti, REST_POOL_Z)),
            pl.BlockSpec((None,) + w_pool_bf16.shape[1:], lambda bi, ti: (layer, 0, 0, 0)),
            pl.BlockSpec((None, 1, BRANCH_WIDTH), lambda bi, ti: (layer, 0, 0)),
        ],
        out_specs=pl.BlockSpec((1, tt, BRANCH_WIDTH), lambda bi, ti: (bi, ti, 0)),
        out_shape=jax.ShapeDtypeStruct((b, s, BRANCH_WIDTH), BF16),
        scratch_shapes=[pltpu.VMEM((tt + POOL_HALO, BRANCH_WIDTH), F32)],
        compiler_params=_params("parallel", "arbitrary"),
        name="multiscale_pool",
    )(rest, rest, w_pool_bf16, pool_scale.reshape(-1, 1, BRANCH_WIDTH))


def _mem_attn_kernel(q_ref, z_ref, mk_ref, mv_ref, o_ref):
    hd = MEM_HEAD_DIM
    for h in range(MEM_HEADS):
        cols = slice(h * hd, (h + 1) * hd)
        q = q_ref[0, :, cols] * (MEM_HEAD_DIM ** -0.5)
        s = _dot_nt(q, mk_ref[0, :, cols].astype(BF16))
        p = jnp.exp(s - jnp.max(s, axis=-1, keepdims=True))
        l = jnp.sum(p, axis=-1, keepdims=True)
        o = _dot(p.astype(BF16), mv_ref[0, :, cols].astype(BF16)) / l
        o_ref[0, :, cols] = (o * _silu(z_ref[0, :, cols])).astype(o_ref.dtype)


def memory_branch(qkv, rest, mkv, tq=2048):
    b, s, _ = qkv.shape
    m = mkv.shape[1]
    tq = _tile(s, tq)
    w = BRANCH_WIDTH
    return pl.pallas_call(
        _mem_attn_kernel,
        grid=(b, s // tq),
        in_specs=[
            pl.BlockSpec((1, tq, w), lambda bi, qi: (bi, qi, QKV_MEM_Q)),
            pl.BlockSpec((1, tq, w), lambda bi, qi: (bi, qi, REST_MEM_Z)),
            pl.BlockSpec((1, m, w), lambda bi, qi: (bi, 0, 0)),
            pl.BlockSpec((1, m, w), lambda bi, qi: (bi, 0, 1)),
        ],
        out_specs=pl.BlockSpec((1, tq, w), lambda bi, qi: (bi, qi, 0)),
        out_shape=jax.ShapeDtypeStruct((b, s, BRANCH_WIDTH), BF16),
        compiler_params=_params("parallel", "parallel"),
        name="memory_attention",
    )(qkv, rest, mkv, mkv)


def _merge_kernel(b0_ref, b1_ref, b2_ref, b3_ref, g0_ref, g1_ref, g2_ref, g3_ref, gb_ref, w_ref, o_ref):
    branches = (b0_ref, b1_ref, b2_ref, b3_ref)
    gates = (g0_ref, g1_ref, g2_ref, g3_ref)
    merged = None
    for n in range(N_BRANCH):
        gate = 1.0 / (1.0 + jnp.exp(-(gates[n][...] + gb_ref[n:n + 1, :])))
        term = gate * _dot(branches[n][...], w_ref[n])
        merged = term if merged is None else merged + term
    o_ref[...] = merged.astype(o_ref.dtype)


def gated_merge(branches, rest2d, gate_b, w_branch_bf16, layer, tm=512, tn=1024):
    m = rest2d.shape[0]
    d = w_branch_bf16.shape[3]
    tm, tn = _tile(m, tm), _tile(d, tn)
    gate_col0 = REST_GATES * BRANCH_WIDTH // tn
    per_gate = d // tn
    br_spec = pl.BlockSpec((tm, BRANCH_WIDTH), lambda j, i: (i, 0))

    def gate_spec(n):
        return pl.BlockSpec((tm, tn), lambda j, i: (i, gate_col0 + n * per_gate + j))

    return pl.pallas_call(
        _merge_kernel,
        grid=(d // tn, m // tm),
        in_specs=[br_spec] * N_BRANCH + [gate_spec(n) for n in range(N_BRANCH)] + [
            pl.BlockSpec((None, N_BRANCH, tn), lambda j, i: (layer, 0, j)),
            pl.BlockSpec((None, N_BRANCH, BRANCH_WIDTH, tn), lambda j, i: (layer, 0, 0, j)),
        ],
        out_specs=pl.BlockSpec((tm, tn), lambda j, i: (i, j)),
        out_shape=jax.ShapeDtypeStruct((m, d), BF16),
        compiler_params=_params("parallel", "parallel"),
        name="gated_merge",
    )(*branches, rest2d, rest2d, rest2d, rest2d, gate_b, w_branch_bf16)


def kernel(x, mem, rel_bias, norm_g, w_in, gate_b, lam_q1, lam_k1, lam_q2, lam_k2, da_norm_g, w_pool,
           pool_scale, mem_norm_g, w_mem_kv, w_branch, w_out, final_g):
    b, s, d = x.shape
    depth = norm_g.shape[0]
    n_mem = mem.shape[1]
    t = min(ATTN_TILE, s)
    hp = ATTN_HEADS_PER_STEP

    w_in_bf, w_pool_bf, w_mem_kv_bf = w_in.astype(BF16), w_pool.astype(BF16), w_mem_kv.astype(BF16)
    w_branch_bf, w_out_bf = w_branch.astype(BF16), w_out.astype(BF16)
    lamv = jnp.stack([lam_q1, lam_k1, lam_q2, lam_k2], axis=1)
    n_gate_blocks = (w_in.shape[2] - N_SLICES * BRANCH_WIDTH) // BRANCH_WIDTH
    rest_blocks = REST_SLICES + tuple(range(N_SLICES, N_SLICES + n_gate_blocks))
    mkv_blocks = tuple(range(w_mem_kv.shape[2] // BRANCH_WIDTH))

    bias = bias_tiles(rel_bias, t)
    x2d = x.reshape(b * s, d)
    mem2d = mem.reshape(b * n_mem, d)
    h = rmsnorm_bf16(x2d, norm_g, 0)
    for l in range(depth):
        lam_init = 0.8 - 0.6 * math.exp(-0.3 * l)
        qkv = matmul_cols(h, w_in_bf, l, QKV_SLICES, BF16).reshape(b, s, -1)
        rest2d = matmul_cols(h, w_in_bf, l, rest_blocks, F32)
        rest = rest2d.reshape(b, s, -1)

        br_da, br_sb = attention_branches(qkv, rest, bias, lamv, da_norm_g, l, lam_init, t, hp)
        br_pool = pool_branch(rest, w_pool_bf, pool_scale, l)
        mkv = matmul_cols(rmsnorm_bf16(mem2d, mem_norm_g, l), w_mem_kv_bf, l, mkv_blocks, F32)
        br_mem = memory_branch(qkv, rest, mkv.reshape(b, n_mem, 2 * BRANCH_WIDTH))

        branches = [br.reshape(b * s, BRANCH_WIDTH) for br in (br_da, br_sb, br_pool, br_mem)]
        merged = gated_merge(branches, rest2d, gate_b, w_branch_bf, l)
        if l == depth - 1:
            (out2d,) = matmul_residual(merged, w_out_bf, l, x2d, final_g, 0, final_norm=True)
        else:
            x2d, h = matmul_residual(merged, w_out_bf, l, x2d, norm_g, l + 1, final_norm=False)
    return out2d.reshape(b, s, d)
```
